```python
import math
import jax, jax.numpy as jnp
from jax import lax
import numpy as np

D_MODEL = 1024
BATCH = 16
SEQ = 2048
DEPTH = 1
DEC_BATCH = 128
DEC_SEQ = 8
PAST_LEN = 8192
PAGE_SIZE = 128

MIX_W = D_MODEL
SB_W = D_MODEL // 2
SB_HD = 64
SB_HEADS = SB_W // SB_HD
DF_W = D_MODEL - SB_W
DF_HD = 64
DF_HEADS = DF_W // (2 * DF_HD)
IN_W = 4 * SB_W + 4 * DF_W
ROT_DIM = DF_HD // 4
ROPE_THETA = 500000.0
Q_BLOCK = 128
NORM_EPS = 1e-6
SUBLN_EPS = 1e-5
NEG_INF = -1e30

kernel_name = "hymba_stickbreak_diffattn_step"


def rmsnorm(x, g, eps=NORM_EPS):
    xf = x.astype(jnp.float32)
    y = xf * lax.rsqrt(jnp.mean(xf * xf, axis=-1, keepdims=True) + eps)
    return (y * g.astype(jnp.float32)).astype(x.dtype)


def partial_rope(x, pos):
    half = ROT_DIM // 2
    inv = ROPE_THETA ** (-jnp.arange(half, dtype=jnp.float32) * 2.0 / ROT_DIM)
    ang = pos.astype(jnp.float32)[:, None] * inv[None, :]
    cos = jnp.cos(ang)[None, :, None, None, :]
    sin = jnp.sin(ang)[None, :, None, None, :]
    xf = x[..., :ROT_DIM].astype(jnp.float32)
    x1, x2 = xf[..., :half], xf[..., half:]
    rot = jnp.concatenate([x1 * cos - x2 * sin, x2 * cos + x1 * sin], axis=-1).astype(x.dtype)
    return jnp.concatenate([rot, x[..., ROT_DIM:]], axis=-1)


def branch_inputs(h, w_in, pos):
    B, T, _ = h.shape
    p = h @ w_in
    idx = [SB_W, 2 * SB_W, 3 * SB_W, 4 * SB_W,
           4 * SB_W + DF_W, 4 * SB_W + 2 * DF_W, 4 * SB_W + 3 * DF_W]
    sq, sk, sv, sg, dq, dk, dv, dg = jnp.split(p, idx, axis=-1)
    sb_q = sq.reshape(B, T, SB_HEADS, SB_HD)
    sb_k = sk.reshape(B, T, SB_HEADS, SB_HD)
    sb_v = sv.reshape(B, T, SB_HEADS, SB_HD)
    df_q = partial_rope(dq.reshape(B, T, DF_HEADS, 2, DF_HD), pos)
    df_k = partial_rope(dk.reshape(B, T, DF_HEADS, 2, DF_HD), pos)
    df_v = dv.reshape(B, T, DF_HEADS, 2 * DF_HD)
    return sb_q, sb_k, sb_v, sg, df_q, df_k, df_v, dg


def stick_breaking(q, k, v, q_pos, k_pos):
    z = jnp.einsum('bqhd,bkhd->bhqk', q, k).astype(jnp.float32) * (SB_HD ** -0.5)
    mask = k_pos[None, :] < q_pos[:, None]
    log_1m = jnp.where(mask, jax.nn.log_sigmoid(-z), 0.0)
    between = lax.cumsum(log_1m, axis=3, reverse=True) - log_1m
    w = jnp.where(mask, jnp.exp(jax.nn.log_sigmoid(z) + between), 0.0)
    return jnp.einsum('bhqk,bkhd->bqhd', w.astype(v.dtype), v)


def diff_attn(q, k, v, q_pos, k_pos, lam, subln_g, lam_init):
    z = jnp.einsum('bqhcd,bkhcd->bhcqk', q, k).astype(jnp.float32) * (DF_HD ** -0.5)
    mask = k_pos[None, :] <= q_pos[:, None]
    p = jax.nn.softmax(jnp.where(mask, z, NEG_INF), axis=-1)
    a = p[:, :, 0] - lam * p[:, :, 1]
    o = jnp.einsum('bhqk,bkhe->bqhe', a.astype(v.dtype), v)
    return rmsnorm(o, subln_g, SUBLN_EPS) * (1.0 - lam_init)


def prompt_mix(sb_q, sb_k, sb_v, df_q, df_k, df_v, lam, subln_g, lam_init):
    B, S = sb_q.shape[0], sb_q.shape[1]
    k_pos = jnp.arange(S)

    def block(i):
        start = i * Q_BLOCK
        q_pos = start + jnp.arange(Q_BLOCK)
        qs = lax.dynamic_slice_in_dim(sb_q, start, Q_BLOCK, axis=1)
        qd = lax.dynamic_slice_in_dim(df_q, start, Q_BLOCK, axis=1)
        return (stick_breaking(qs, sb_k, sb_v, q_pos, k_pos),
                diff_attn(qd, df_k, df_v, q_pos, k_pos, lam, subln_g, lam_init))

    o_sb, o_df = lax.map(block, jnp.arange(S // Q_BLOCK))
    o_sb = jnp.moveaxis(o_sb, 0, 1).reshape(B, S, SB_W)
    o_df = jnp.moveaxis(o_df, 0, 1).reshape(B, S, DF_W)
    return o_sb, o_df


def sample_mix(sb_q, sb_k, sb_v, df_q, df_k, df_v, c_sb_k, c_sb_v, c_df_k, c_df_v,
               page_table, lam, subln_g, lam_init):
    Bd, T = sb_q.shape[0], sb_q.shape[1]
    past = page_table.shape[1] * c_sb_k.shape[1]
    k_pos = jnp.arange(past + T)
    q_pos = past + jnp.arange(T)

    def gather(c, pt):
        return c[pt].reshape((past,) + c.shape[2:])

    def per_seq(args):
        pt, sq, sk, sv, dq, dk, dv = args
        ksb = jnp.concatenate([gather(c_sb_k, pt), sk], axis=0)[None]
        vsb = jnp.concatenate([gather(c_sb_v, pt), sv], axis=0)[None]
        kdf = jnp.concatenate([gather(c_df_k, pt), dk], axis=0)[None]
        vdf = jnp.concatenate([gather(c_df_v, pt), dv], axis=0)[None]
        o1 = stick_breaking(sq[None], ksb, vsb, q_pos, k_pos)[0]
        o2 = diff_attn(dq[None], kdf, vdf, q_pos, k_pos, lam, subln_g, lam_init)[0]
        return o1, o2

    o_sb, o_df = lax.map(per_seq, (page_table, sb_q, sb_k, sb_v, df_q, df_k, df_v))
    return o_sb.reshape(Bd, T, SB_W), o_df.reshape(Bd, T, DF_W)


def setup_inputs(seed: int = 0) -> dict:
    key = jax.random.key(seed)
    ks = jax.random.split(key, 20)
    f32 = jnp.float32
    n_pages = PAST_LEN // PAGE_SIZE
    n_used = DEC_BATCH * n_pages
    n_pool = n_used + max(1, n_used // 4)
    perm = jax.random.permutation(ks[0], n_pool)
    page_table = perm[:n_used].reshape(DEC_BATCH, n_pages).astype(jnp.int32)
    return {
        "x_prompt": jax.random.normal(ks[1], (BATCH, SEQ, D_MODEL), f32),
        "x_sample": jax.random.normal(ks[2], (DEC_BATCH, DEC_SEQ, D_MODEL), f32),
        "cache_sb_k": jax.random.normal(ks[3], (DEPTH, n_pool, PAGE_SIZE, SB_HEADS, SB_HD), f32),
        "cache_sb_v": jax.random.normal(ks[4], (DEPTH, n_pool, PAGE_SIZE, SB_HEADS, SB_HD), f32),
        "cache_df_k": jax.random.normal(ks[5], (DEPTH, n_pool, PAGE_SIZE, DF_HEADS, 2, DF_HD), f32),
        "cache_df_v": jax.random.normal(ks[6], (DEPTH, n_pool, PAGE_SIZE, DF_HEADS, 2 * DF_HD), f32),
        "page_table": page_table,
        "norm_in_g": 1.0 + 0.02 * jax.random.normal(ks[7], (DEPTH, D_MODEL), f32),
        "w_in": jax.random.normal(ks[8], (DEPTH, D_MODEL, IN_W), f32) * D_MODEL ** -0.5,
        "lambda_q1": 0.1 * jax.random.normal(ks[9], (DEPTH, DF_HD), f32),
        "lambda_k1": 0.1 * jax.random.normal(ks[10], (DEPTH, DF_HD), f32),
        "lambda_q2": 0.1 * jax.random.normal(ks[11], (DEPTH, DF_HD), f32),
        "lambda_k2": 0.1 * jax.random.normal(ks[12], (DEPTH, DF_HD), f32),
        "subln_g": 1.0 + 0.02 * jax.random.normal(ks[13], (DEPTH, 2 * DF_HD), f32),
        "w_out": jax.random.normal(ks[14], (DEPTH, MIX_W, D_MODEL), f32) * MIX_W ** -0.5,
        "norm_f_g": 1.0 + 0.02 * jax.random.normal(ks[15], (D_MODEL,), f32),
    }


def reference(x_prompt, x_sample, cache_sb_k, cache_sb_v, cache_df_k, cache_df_v, page_table,
              norm_in_g, w_in, lambda_q1, lambda_k1, lambda_q2, lambda_k2, subln_g, w_out,
              norm_f_g):
    pos_p = jnp.arange(x_prompt.shape[1])
    pos_s = PAST_LEN + jnp.arange(x_sample.shape[1])
    xp, xs = x_prompt, x_sample
    pk_sb, pv_sb, pk_df, pv_df = [], [], [], []
    sk_sb, sv_sb, sk_df, sv_df = [], [], [], []
    for l in range(DEPTH):
        lam_init = 0.8 - 0.6 * math.exp(-0.3 * l)
        lam = (jnp.exp(jnp.sum(lambda_q1[l].astype(jnp.float32) * lambda_k1[l].astype(jnp.float32)))
               - jnp.exp(jnp.sum(lambda_q2[l].astype(jnp.float32) * lambda_k2[l].astype(jnp.float32)))
               + lam_init)

        hp = rmsnorm(xp, norm_in_g[l])
        sq, sk, sv, sg, dq, dk, dv, dg = branch_inputs(hp, w_in[l], pos_p)
        o_sb, o_df = prompt_mix(sq, sk, sv, dq, dk, dv, lam, subln_g[l], lam_init)
        mix = jnp.concatenate([o_sb * jax.nn.silu(sg), o_df * jax.nn.silu(dg)], axis=-1)
        xp = xp + mix @ w_out[l]
        pk_sb.append(sk); pv_sb.append(sv); pk_df.append(dk); pv_df.append(dv)

        hs = rmsnorm(xs, norm_in_g[l])
        sq, sk, sv, sg, dq, dk, dv, dg = branch_inputs(hs, w_in[l], pos_s)
        o_sb, o_df = sample_mix(sq, sk, sv, dq, dk, dv, cache_sb_k[l], cache_sb_v[l],
                                cache_df_k[l], cache_df_v[l], page_table, lam, subln_g[l], lam_init)
        mix = jnp.concatenate([o_sb * jax.nn.silu(sg), o_df * jax.nn.silu(dg)], axis=-1)
        xs = xs + mix @ w_out[l]
        sk_sb.append(sk); sv_sb.append(sv); sk_df.append(dk); sv_df.append(dv)

    y_prompt = rmsnorm(xp, norm_f_g)
    y_sample = rmsnorm(xs, norm_f_g)
    return (y_prompt, y_sample,
            jnp.stack(pk_sb), jnp.stack(pv_sb), jnp.stack(pk_df), jnp.stack(pv_df),
            jnp.stack(sk_sb), jnp.stack(sv_sb), jnp.stack(sk_df), jnp.stack(sv_df))
```

```python
import functools
import math

import jax
import jax.numpy as jnp
from jax import lax
from jax.experimental import pallas as pl
from jax.experimental.pallas import tpu as pltpu

F32 = jnp.float32
BF16 = jnp.bfloat16

HEAD_DIM = 64
SLAB = 128
GROUP_W = 512
ROT_DIM = 16
ROT_HALF = ROT_DIM // 2
ROPE_THETA = 500000.0
NORM_EPS = 1e-6
SUBLN_EPS = 1e-5
NEG_INF = -1e30
LAM_INIT = 0.8 - 0.6 * math.exp(-0.3 * 0)
QK_SCALE = HEAD_DIM ** -0.5

PROJ_ROWS = 512
OUT_ROWS = 512
Q_ROWS = 256
K_ROWS = 256
CHUNK_PAGES = 2
VMEM_LIMIT = 56 * 1024 * 1024

NT_DIMS = (((1,), (1,)), ((), ()))


def _softplus(z):
    return jnp.maximum(z, 0.0) + jnp.log(1.0 + jnp.exp(-jnp.abs(z)))


def _silu(x):
    return x / (1.0 + jnp.exp(-x))


def _tri(n):
    j = lax.broadcasted_iota(jnp.int32, (n, n), 0)
    s = lax.broadcasted_iota(jnp.int32, (n, n), 1)
    return jnp.where(j >= s, 1.0, 0.0).astype(BF16)


def _rev_cumsum(x, tri):
    hi = x.astype(BF16)
    lo = (x - hi.astype(F32)).astype(BF16)
    return (jnp.dot(hi, tri, preferred_element_type=F32)
            + jnp.dot(lo, tri, preferred_element_type=F32))


def _lam(lam_ref):
    lv = lam_ref[...]
    s1 = jnp.sum(lv[0:1] * lv[1:2], axis=1, keepdims=True)
    s2 = jnp.sum(lv[2:3] * lv[3:4], axis=1, keepdims=True)
    return jnp.exp(s1) - jnp.exp(s2) + LAM_INIT


def _normed(x_ref, g_ref):
    x = x_ref[...]
    ms = jnp.mean(x * x, axis=-1, keepdims=True)
    return ((x * lax.rsqrt(ms + NORM_EPS)) * g_ref[...]).astype(BF16)


def _rope_rows(p, cos, sin_lo, sin_hi):
    slabs = []
    for s in range(p.shape[1] // SLAB):
        xs = p[:, s * SLAB:(s + 1) * SLAB]
        slabs.append(xs * cos + pltpu.roll(xs, SLAB - ROT_HALF, axis=1) * sin_lo
                     + pltpu.roll(xs, ROT_HALF, axis=1) * sin_hi)
    return jnp.concatenate(slabs, axis=1)


def _proj_prompt_kernel(x_ref, g_ref, w_ref, wt_ref, cos_ref, sin_lo_ref, sin_hi_ref,
                        cos_t_ref, sin_t_ref,
                        skt_f, svt_f, dkt_f, dv_f, sq, skt, sv, dq, dkt, dv, gate):
    h = _normed(x_ref, g_ref)

    def proj(j):
        return jnp.dot(h, w_ref[:, j * GROUP_W:(j + 1) * GROUP_W], preferred_element_type=F32)

    def proj_t(j):
        return lax.dot_general(wt_ref[j * GROUP_W:(j + 1) * GROUP_W, :], h, NT_DIMS,
                               preferred_element_type=F32)

    def store_blocks(ref, kt):
        for kb in range(kt.shape[1] // K_ROWS):
            ref[0, kb] = kt[:, kb * K_ROWS:(kb + 1) * K_ROWS].astype(BF16)

    sq[...] = (proj(0) * QK_SCALE).astype(BF16)
    kt = proj_t(0)
    skt_f[0] = kt
    store_blocks(skt, kt)
    svt_f[0] = proj_t(1)
    sv[...] = proj(1).astype(BF16)
    gate[:, 0:GROUP_W] = _silu(proj(2)).astype(BF16)

    dq[...] = (_rope_rows(proj(3), cos_ref[...], sin_lo_ref[...], sin_hi_ref[...])
               * QK_SCALE).astype(BF16)
    kt = proj_t(2)
    cos_t, sin_t = cos_t_ref[...], sin_t_ref[...]
    parts = []
    for sh in range(GROUP_W // HEAD_DIM):
        base = sh * HEAD_DIM
        x1 = kt[base:base + ROT_HALF]
        x2 = kt[base + ROT_HALF:base + ROT_DIM]
        parts += [x1 * cos_t - x2 * sin_t, x2 * cos_t + x1 * sin_t,
                  kt[base + ROT_DIM:base + HEAD_DIM]]
    kt = jnp.concatenate(parts, axis=0)
    dkt_f[0] = kt
    store_blocks(dkt, kt)
    v = proj(4)
    dv_f[...] = v
    dv[...] = v.astype(BF16)
    gate[:, GROUP_W:2 * GROUP_W] = _silu(proj(5)).astype(BF16)


def _proj_sample_kernel(x_ref, g_ref, w_ref, cos_ref, sin_lo_ref, sin_hi_ref,
                        sq, sk, sv, dq, dk, dv, gate):
    h = _normed(x_ref, g_ref)

    def proj(j):
        return jnp.dot(h, w_ref[:, j * GROUP_W:(j + 1) * GROUP_W], preferred_element_type=F32)

    rope = lambda p: _rope_rows(p, cos_ref[...], sin_lo_ref[...], sin_hi_ref[...])
    sq[...] = proj(0) * QK_SCALE
    sk[...] = proj(1)
    sv[...] = proj(2)
    gate[:, 0:GROUP_W] = _silu(proj(3))
    dq[...] = rope(proj(4)) * QK_SCALE
    dk[...] = rope(proj(5))
    dv[...] = proj(6)
    gate[:, GROUP_W:2 * GROUP_W] = _silu(proj(7))


def _rope_angles(pos):
    inv = ROPE_THETA ** (-jnp.arange(ROT_HALF, dtype=F32) * 2.0 / ROT_DIM)
    ang = pos.astype(F32)[:, None] * inv[None, :]
    return jnp.cos(ang), jnp.sin(ang)


def _rope_tables(pos):
    cos, sin = _rope_angles(pos)
    n = pos.shape[0]
    ones = jnp.ones((n, HEAD_DIM - ROT_DIM), F32)
    zeros = jnp.zeros((n, HEAD_DIM - ROT_DIM), F32)
    zh = jnp.zeros((n, ROT_HALF), F32)
    cos64 = jnp.concatenate([cos, cos, ones], axis=1)
    lo64 = jnp.concatenate([-sin, zh, zeros], axis=1)
    hi64 = jnp.concatenate([zh, sin, zeros], axis=1)
    dup = lambda t: jnp.concatenate([t, t], axis=1)
    return dup(cos64), dup(lo64), dup(hi64)


def _project_prompt(x2d, norm_g, w_in, batch, seq):
    m = x2d.shape[0]
    tm = PROJ_ROWS
    nb = seq // tm
    nkb = seq // K_ROWS
    g = lambda j: w_in[:, j * GROUP_W:(j + 1) * GROUP_W]
    w_rows = jnp.concatenate([g(0), g(2), g(3), g(4), g(6), g(7)], axis=1).astype(BF16)
    w_t = jnp.concatenate([g(1), g(2), g(5)], axis=1).T.astype(BF16)
    pos = jnp.arange(seq)
    cos, sin = _rope_angles(pos)
    row = lambda i: (i, 0)
    full = lambda i: (0, 0)
    tab = pl.BlockSpec((tm, SLAB), lambda i: (i % nb, 0))
    tab_t = pl.BlockSpec((ROT_HALF, tm), lambda i: (0, i % nb))
    feat_major = jax.ShapeDtypeStruct((batch, GROUP_W, seq), F32)
    feat_spec = pl.BlockSpec((1, GROUP_W, tm), lambda i: (i // nb, 0, i % nb))
    kt_shape = jax.ShapeDtypeStruct((batch, nkb, GROUP_W, K_ROWS), BF16)
    kt_spec = pl.BlockSpec((1, tm // K_ROWS, GROUP_W, K_ROWS), lambda i: (i // nb, i % nb, 0, 0))
    b512 = jax.ShapeDtypeStruct((m, GROUP_W), BF16)
    blk512 = pl.BlockSpec((tm, GROUP_W), row)
    return pl.pallas_call(
        _proj_prompt_kernel,
        grid=(m // tm,),
        in_specs=[pl.BlockSpec((tm, 1024), row),
                  pl.BlockSpec((1, 1024), full),
                  pl.BlockSpec(w_rows.shape, full),
                  pl.BlockSpec(w_t.shape, full),
                  tab, tab, tab, tab_t, tab_t],
        out_specs=[feat_spec, feat_spec, feat_spec, blk512,
                   blk512, kt_spec, blk512, blk512, kt_spec, blk512,
                   pl.BlockSpec((tm, 2 * GROUP_W), row)],
        out_shape=[feat_major, feat_major, feat_major, jax.ShapeDtypeStruct((m, GROUP_W), F32),
                   b512, kt_shape, b512, b512, kt_shape, b512,
                   jax.ShapeDtypeStruct((m, 2 * GROUP_W), BF16)],
        compiler_params=pltpu.CompilerParams(
            dimension_semantics=("arbitrary",), vmem_limit_bytes=VMEM_LIMIT),
        name="proj_prompt",
    )(x2d, norm_g, w_rows, w_t, *_rope_tables(pos), cos.T, sin.T)


def _project_sample(x2d, norm_g, w_in, pos):
    m = x2d.shape[0]
    tm = PROJ_ROWS
    row = lambda i: (i, 0)
    full = lambda i: (0, 0)
    tab = pl.BlockSpec((tm, SLAB), full)
    f512 = jax.ShapeDtypeStruct((m, GROUP_W), F32)
    blk512 = pl.BlockSpec((tm, GROUP_W), row)
    return pl.pallas_call(
        _proj_sample_kernel,
        grid=(m // tm,),
        in_specs=[pl.BlockSpec((tm, 1024), row),
                  pl.BlockSpec((1, 1024), full),
                  pl.BlockSpec((1024, 8 * GROUP_W), full),
                  tab, tab, tab],
        out_specs=[blk512] * 6 + [pl.BlockSpec((tm, 2 * GROUP_W), row)],
        out_shape=[f512] * 6 + [jax.ShapeDtypeStruct((m, 2 * GROUP_W), F32)],
        compiler_params=pltpu.CompilerParams(
            dimension_semantics=("arbitrary",), vmem_limit_bytes=VMEM_LIMIT),
        name="proj_sample",
    )(x2d, norm_g, w_in.astype(BF16), *_rope_tables(pos))


def _prompt_kernel(qs_ref, kts_ref, vs_ref, qd_ref, ktd_ref, vd_ref, gs_ref, gd_ref,
                   lam_ref, subg_ref, os_ref, od_ref):
    tq, tk = Q_ROWS, K_ROWS
    i = pl.program_id(2)
    kd = (i * tq) // tk
    lane = lax.broadcasted_iota(jnp.int32, (tq, SLAB), 1)
    first = lane < HEAD_DIM
    q_pos = i * tq + lax.broadcasted_iota(jnp.int32, (tq, tk), 0)
    k_pos = kd * tk + lax.broadcasted_iota(jnp.int32, (tq, tk), 1)
    tri = _tri(tk)
    zero_q = jnp.zeros((tq, SLAB), BF16)

    def keys(ref, kb):
        return ref[0, pl.ds(pl.multiple_of(kb * tk, tk), tk), :]

    def sb_step(qh, kb, carry, acc, mask):
        z = jnp.dot(qh, kts_ref[0, kb], preferred_element_type=F32)
        sp = _softplus(z)
        if mask is not None:
            sp = jnp.where(mask, sp, 0.0)
        cs = _rev_cumsum(sp, tri)
        w = jnp.exp(z - (cs + carry))
        if mask is not None:
            w = jnp.where(mask, w, 0.0)
        acc = acc + jnp.dot(w.astype(BF16), keys(vs_ref, kb), preferred_element_type=F32)
        return carry + cs[:, 0:1], acc

    qs = qs_ref[0]
    sb_out = []
    for half in range(2):
        qh = jnp.where(first if half == 0 else ~first, qs, zero_q)
        carry, acc = sb_step(qh, kd, jnp.zeros((tq, 1), F32), jnp.zeros((tq, SLAB), F32),
                             k_pos < q_pos)

        def sb_body(it, st, qh=qh):
            return sb_step(qh, kd - 1 - it, st[0], st[1], None)

        carry, acc = lax.fori_loop(0, kd, sb_body, (carry, acc))
        sb_out.append(acc)
    o_sb = jnp.where(first, sb_out[0], sb_out[1])
    os_ref[0] = (o_sb * gs_ref[0].astype(F32)).astype(os_ref.dtype)

    def df_step(qc, kb, m, l, acc, mask):
        z = jnp.dot(qc, ktd_ref[0, kb], preferred_element_type=F32)
        if mask is not None:
            z = jnp.where(mask, z, NEG_INF)
        m_new = jnp.maximum(m, jnp.max(z, axis=1, keepdims=True))
        p = jnp.exp(z - m_new)
        alpha = jnp.exp(m - m_new)
        l = alpha * l + jnp.sum(p, axis=1, keepdims=True)
        acc = alpha * acc + jnp.dot(p.astype(BF16), keys(vd_ref, kb), preferred_element_type=F32)
        return m_new, l, acc

    qd = qd_ref[0]
    df_out = []
    for sub in range(2):
        qc = jnp.where(first if sub == 0 else ~first, qd, zero_q)
        st = df_step(qc, kd, jnp.full((tq, 1), NEG_INF, F32), jnp.zeros((tq, 1), F32),
                     jnp.zeros((tq, SLAB), F32), k_pos <= q_pos)

        def df_body(it, st, qc=qc):
            return df_step(qc, kd - 1 - it, st[0], st[1], st[2], None)

        m, l, acc = lax.fori_loop(0, kd, df_body, st)
        df_out.append(acc / l)
    o = df_out[0] - _lam(lam_ref) * df_out[1]
    o = (o * lax.rsqrt(jnp.mean(o * o, axis=-1, keepdims=True) + SUBLN_EPS)) * subg_ref[...]
    o = o * (1.0 - LAM_INIT)
    od_ref[0] = (o * gd_ref[0].astype(F32)).astype(od_ref.dtype)


def _prompt_attention(q_sb, kt_sb, v_sb, q_df, kt_df, v_df, gate, lamv, subg, batch, seq):
    tq = Q_ROWS
    nkb = seq // K_ROWS
    r3 = lambda a: a.reshape(batch, seq, a.shape[-1])
    qspec = pl.BlockSpec((1, tq, SLAB), lambda b, j, i: (b, i, j))
    ktspec = pl.BlockSpec((1, nkb, SLAB, K_ROWS), lambda b, j, i: (b, 0, j, 0))
    vspec = pl.BlockSpec((1, seq, SLAB), lambda b, j, i: (b, 0, j))
    n_slab = GROUP_W // SLAB
    out = jax.ShapeDtypeStruct((batch, seq, GROUP_W), BF16)
    return pl.pallas_call(
        _prompt_kernel,
        grid=(batch, n_slab, seq // tq),
        in_specs=[qspec, ktspec, vspec, qspec, ktspec, vspec,
                  qspec,
                  pl.BlockSpec((1, tq, SLAB), lambda b, j, i: (b, i, n_slab + j)),
                  pl.BlockSpec((4, HEAD_DIM), lambda b, j, i: (0, 0)),
                  pl.BlockSpec((1, SLAB), lambda b, j, i: (0, 0))],
        out_specs=[qspec, qspec],
        out_shape=[out, out],
        compiler_params=pltpu.CompilerParams(
            dimension_semantics=("arbitrary", "arbitrary", "arbitrary"),
            vmem_limit_bytes=VMEM_LIMIT),
        name="prompt_attn",
    )(r3(q_sb), kt_sb, r3(v_sb), r3(q_df), kt_df, r3(v_df), r3(gate), r3(gate), lamv, subg)


def _sample_kernel(pt_ref, qs_ref, qd_ref, ksn_ref, vsn_ref, kdn_ref, vdn_ref, gate_ref,
                   lam_ref, subg_ref, c_sk, c_sv, c_dk, c_dv, os_ref, od_ref,
                   kt_sb, vt_sb, kt_df, v_df, sem, acc_sb, acc_df, *, n_seq, n_pages, page):
    g_pages = CHUNK_PAGES
    n_chunks = n_pages // g_pages
    ck = g_pages * page
    n_tok = qs_ref.shape[1]
    n_grp = GROUP_W // HEAD_DIM
    n_dfh = GROUP_W // SLAB
    rows = n_grp * n_tok
    s = pl.program_id(0)

    def chunk_copies(seq, c, slot):
        base = n_pages - (c + 1) * g_pages
        for g in range(g_pages):
            pg = pt_ref[seq, base + g]
            lanes = pl.ds(g * page, page)
            yield pltpu.make_async_copy(c_sk.at[pg], kt_sb.at[slot, :, lanes], sem.at[slot, 0, g])
            yield pltpu.make_async_copy(c_sv.at[pg], vt_sb.at[slot, :, lanes], sem.at[slot, 1, g])
            yield pltpu.make_async_copy(c_dk.at[pg], kt_df.at[slot, :, lanes], sem.at[slot, 2, g])
            yield pltpu.make_async_copy(c_dv.at[pg],
                                        v_df.at[slot, pl.ds(g * page * n_dfh, page * n_dfh), :],
                                        sem.at[slot, 3, g])

    def start_chunk(seq, c, slot):
        for cp in chunk_copies(seq, c, slot):
            cp.start()

    def wait_chunk(seq, c, slot):
        for cp in chunk_copies(seq, c, slot):
            cp.wait()

    @pl.when(s == 0)
    def _():
        start_chunk(0, 0, 0)

    row = lax.broadcasted_iota(jnp.int32, (rows, GROUP_W), 0)
    lane = lax.broadcasted_iota(jnp.int32, (rows, GROUP_W), 1)
    own64 = (row // n_tok) == (lane // HEAD_DIM)

    def block_diag(q_ref):
        q = q_ref[0]
        qt = jnp.concatenate([q] * n_grp, axis=0)
        return jnp.where(own64, qt, 0.0).astype(BF16)

    q_sb = block_diag(qs_ref)
    q_df = block_diag(qd_ref)

    def sb_weights(z, tri, carry, mask):
        sp = _softplus(z)
        if mask is not None:
            sp = jnp.where(mask, sp, 0.0)
        cs = _rev_cumsum(sp, tri)
        w = jnp.exp(z - (cs + carry))
        if mask is not None:
            w = jnp.where(mask, w, 0.0)
        return w.astype(BF16), carry + cs[:, 0:1]

    def df_weights(z, m, l, mask):
        if mask is not None:
            z = jnp.where(mask, z, NEG_INF)
        m_new = jnp.maximum(m, jnp.max(z, axis=1, keepdims=True))
        p = jnp.exp(z - m_new)
        alpha = jnp.exp(m - m_new)
        l = alpha * l + jnp.sum(p, axis=1, keepdims=True)
        acc_df[...] = alpha * acc_df[...]
        return p.astype(BF16), m_new, l

    def df_values(p, head_values):
        for hd in range(n_dfh):
            r = pl.ds(hd * 2 * n_tok, 2 * n_tok)
            acc_df[r, :] += jnp.dot(p[hd * 2 * n_tok:(hd + 1) * 2 * n_tok], head_values(hd),
                                    preferred_element_type=F32)

    def padded(ref):
        return jnp.concatenate([ref[0], jnp.zeros((page - n_tok, GROUP_W), F32)],
                               axis=0).astype(BF16)

    acc_sb[...] = jnp.zeros_like(acc_sb)
    acc_df[...] = jnp.zeros_like(acc_df)
    t_q = lax.broadcasted_iota(jnp.int32, (rows, page), 0) % n_tok
    s_k = lax.broadcasted_iota(jnp.int32, (rows, page), 1)
    z = lax.dot_general(q_sb, padded(ksn_ref), NT_DIMS, preferred_element_type=F32)
    w, carry = sb_weights(z, _tri(page), jnp.zeros((rows, 1), F32), s_k < t_q)
    acc_sb[...] += jnp.dot(w, padded(vsn_ref), preferred_element_type=F32)
    z = lax.dot_general(q_df, padded(kdn_ref), NT_DIMS, preferred_element_type=F32)
    p, m, l = df_weights(z, jnp.full((rows, 1), NEG_INF, F32), jnp.zeros((rows, 1), F32),
                         s_k <= t_q)
    v_new = padded(vdn_ref)
    df_values(p, lambda hd: v_new[:, hd * SLAB:(hd + 1) * SLAB])

    tri = _tri(ck)

    def body(c, st):
        carry, m, l = st
        slot = c % 2
        wait_chunk(s, c, slot)

        @pl.when(c + 1 < n_chunks)
        def _():
            start_chunk(s, c + 1, 1 - slot)

        @pl.when(jnp.logical_and(c + 1 == n_chunks, s + 1 < n_seq))
        def _():
            start_chunk(s + 1, 0, 1 - slot)

        z = jnp.dot(q_sb, kt_sb[slot].astype(BF16), preferred_element_type=F32)
        w, carry = sb_weights(z, tri, carry, None)
        acc_sb[...] += lax.dot_general(w, vt_sb[slot].astype(BF16), NT_DIMS,
                                       preferred_element_type=F32)
        z = jnp.dot(q_df, kt_df[slot].astype(BF16), preferred_element_type=F32)
        p, m, l = df_weights(z, m, l, None)
        df_values(p, lambda hd: v_df[slot, pl.ds(hd, ck, stride=n_dfh), :].astype(BF16))
        return carry, m, l

    carry, m, l = lax.fori_loop(0, n_chunks, body, (carry, m, l))

    gate = gate_ref[0]
    o_full = jnp.where(own64, acc_sb[...], 0.0)
    o_sb = o_full[0:n_tok]
    for hd in range(1, n_grp):
        o_sb = o_sb + o_full[hd * n_tok:(hd + 1) * n_tok]
    os_ref[0] = o_sb * gate[:, 0:GROUP_W]

    o_all = acc_df[...] / l
    lam = _lam(lam_ref)
    slabs = []
    for hd in range(n_dfh):
        r0 = hd * 2 * n_tok
        oh = o_all[r0:r0 + n_tok] - lam * o_all[r0 + n_tok:r0 + 2 * n_tok]
        oh = (oh * lax.rsqrt(jnp.mean(oh * oh, axis=-1, keepdims=True) + SUBLN_EPS)) * subg_ref[...]
        slabs.append(oh * (1.0 - LAM_INIT))
    od_ref[0] = jnp.concatenate(slabs, axis=1) * gate[:, GROUP_W:2 * GROUP_W]


def _sample_attention(page_table, q_sb, q_df, k_sb, v_sb, k_df, v_df, gate, lamv, subg,
                      c_sk, c_sv, c_dk, c_dv, n_seq, n_tok):
    n_pages = page_table.shape[1]
    page = c_sk.shape[2]
    n_chunks = n_pages // CHUNK_PAGES
    assert n_pages % CHUNK_PAGES == 0 and n_chunks % 2 == 0
    ck = CHUNK_PAGES * page
    r3 = lambda a: a.reshape(n_seq, n_tok, a.shape[-1])
    tok = pl.BlockSpec((1, n_tok, GROUP_W), lambda s, pt: (s, 0, 0))
    hbm = pl.BlockSpec(memory_space=pl.ANY)
    rows = GROUP_W // HEAD_DIM * n_tok
    out = jax.ShapeDtypeStruct((n_seq, n_tok, GROUP_W), F32)
    grid_spec = pltpu.PrefetchScalarGridSpec(
        num_scalar_prefetch=1,
        grid=(n_seq,),
        in_specs=[tok, tok, tok, tok, tok, tok,
                  pl.BlockSpec((1, n_tok, 2 * GROUP_W), lambda s, pt: (s, 0, 0)),
                  pl.BlockSpec((4, HEAD_DIM), lambda s, pt: (0, 0)),
                  pl.BlockSpec((1, SLAB), lambda s, pt: (0, 0)),
                  hbm, hbm, hbm, hbm],
        out_specs=[tok, tok],
        scratch_shapes=[pltpu.VMEM((2, GROUP_W, ck), F32),
                        pltpu.VMEM((2, GROUP_W, ck), F32),
                        pltpu.VMEM((2, GROUP_W, ck), F32),
                        pltpu.VMEM((2, ck * (GROUP_W // SLAB), SLAB), F32),
                        pltpu.SemaphoreType.DMA((2, 4, CHUNK_PAGES)),
                        pltpu.VMEM((rows, GROUP_W), F32),
                        pltpu.VMEM((rows, SLAB), F32)])
    return pl.pallas_call(
        functools.partial(_sample_kernel, n_seq=n_seq, n_pages=n_pages, page=page),
        grid_spec=grid_spec,
        out_shape=[out, out],
        compiler_params=pltpu.CompilerParams(
            dimension_semantics=("arbitrary",), vmem_limit_bytes=VMEM_LIMIT),
        name="sample_attn",
    )(page_table, r3(q_sb), r3(q_df), r3(k_sb), r3(v_sb), r3(k_df), r3(v_df), r3(gate),
      lamv, subg, c_sk, c_sv, c_dk, c_dv)


def _out_kernel(ms_ref, md_ref, x_ref, w_ref, g_ref, y_ref):
    acc = jnp.dot(ms_ref[...].astype(BF16), w_ref[0:GROUP_W, :], preferred_element_type=F32)
    acc = acc + jnp.dot(md_ref[...].astype(BF16), w_ref[GROUP_W:2 * GROUP_W, :],
                        preferred_element_type=F32)
    xo = x_ref[...] + acc
    ms = jnp.mean(xo * xo, axis=-1, keepdims=True)
    y_ref[...] = (xo * lax.rsqrt(ms + NORM_EPS)) * g_ref[...]


def _out_project(mix_sb, mix_df, x2d, w_bf16, norm_g):
    m = x2d.shape[0]
    tm = OUT_ROWS
    row = lambda i: (i, 0)
    return pl.pallas_call(
        _out_kernel,
        grid=(m // tm,),
        in_specs=[pl.BlockSpec((tm, GROUP_W), row), pl.BlockSpec((tm, GROUP_W), row),
                  pl.BlockSpec((tm, 1024), row),
                  pl.BlockSpec((1024, 1024), lambda i: (0, 0)),
                  pl.BlockSpec((1, 1024), lambda i: (0, 0))],
        out_specs=pl.BlockSpec((tm, 1024), row),
        out_shape=jax.ShapeDtypeStruct((m, 1024), F32),
        compiler_params=pltpu.CompilerParams(
            dimension_semantics=("arbitrary",), vmem_limit_bytes=VMEM_LIMIT),
        name="out_proj",
    )(mix_sb, mix_df, x2d, w_bf16, norm_g)


def kernel(x_prompt, x_sample, cache_sb_k, cache_sb_v, cache_df_k, cache_df_v, page_table,
           norm_in_g, w_in, lambda_q1, lambda_k1, lambda_q2, lambda_k2, subln_g, w_out,
           norm_f_g):
    batch, seq, d = x_prompt.shape
    n_seq, n_tok, _ = x_sample.shape
    assert norm_in_g.shape[0] == 1, "single layer"
    n_pool, page = cache_sb_k.shape[1], cache_sb_k.shape[2]
    past = page_table.shape[1] * page

    w_out_b = w_out[0].astype(BF16)
    g_in = norm_in_g[0].reshape(1, d)
    g_f = norm_f_g.reshape(1, d)
    lamv = jnp.stack([lambda_q1[0], lambda_k1[0], lambda_q2[0], lambda_k2[0]])
    subg = subln_g[0].reshape(1, SLAB)

    xp = x_prompt.reshape(batch * seq, d)
    (p_skt, p_svt, p_dkt, p_dv, q_sb, kt_sb, v_sb, q_df, kt_df, v_df, gate) = _project_prompt(
        xp, g_in, w_in[0], batch, seq)
    mix_sb, mix_df = _prompt_attention(q_sb, kt_sb, v_sb, q_df, kt_df, v_df, gate, lamv, subg,
                                       batch, seq)
    y_prompt = _out_project(mix_sb.reshape(batch * seq, GROUP_W),
                            mix_df.reshape(batch * seq, GROUP_W),
                            xp, w_out_b, g_f).reshape(batch, seq, d)
    new_sb_k = jnp.transpose(p_skt.reshape(batch, 8, 64, seq), (0, 3, 1, 2))[None]
    new_sb_v = jnp.transpose(p_svt.reshape(batch, 8, 64, seq), (0, 3, 1, 2))[None]
    new_df_k = jnp.transpose(p_dkt.reshape(batch, 4, 2, 64, seq), (0, 4, 1, 2, 3))[None]
    new_df_v = p_dv.reshape(1, batch, seq, 4, 128)

    xs = x_sample.reshape(n_seq * n_tok, d)
    pos_s = past + jnp.arange(PROJ_ROWS) % n_tok
    (sq_sb, s_sk, s_sv, sq_df, s_dk, s_dv, s_gate) = _project_sample(xs, g_in, w_in[0], pos_s)
    c_sk = jnp.transpose(cache_sb_k[0], (0, 2, 3, 1)).reshape(n_pool, GROUP_W, page)
    c_sv = jnp.transpose(cache_sb_v[0], (0, 2, 3, 1)).reshape(n_pool, GROUP_W, page)
    c_dk = jnp.transpose(cache_df_k[0], (0, 2, 3, 4, 1)).reshape(n_pool, GROUP_W, page)
    c_dv = cache_df_v[0].reshape(n_pool, page * 4, SLAB)
    smix_sb, smix_df = _sample_attention(
        page_table, sq_sb, sq_df, s_sk, s_sv, s_dk, s_dv, s_gate, lamv, subg,
        c_sk, c_sv, c_dk, c_dv, n_seq, n_tok)
    y_sample = _out_project(smix_sb.reshape(n_seq * n_tok, GROUP_W),
                            smix_df.reshape(n_seq * n_tok, GROUP_W),
                            xs, w_out_b, g_f).reshape(n_seq, n_tok, d)

    return (y_prompt, y_sample, new_sb_k, new_sb_v, new_df_k, new_df_v,
            s_sk.reshape(1, n_seq, n_tok, 8, 64), s_sv.reshape(1, n_seq, n_tok, 8, 64),
            s_dk.reshape(1, n_seq, n_tok, 4, 2, 64), s_dv.reshape(1, n_seq, n_tok, 4, 128))
```

```python
import functools
import math

import jax
import jax.numpy as jnp
from jax import lax
from jax.experimental import pallas as pl
from jax.experimental.pallas import tpu as pltpu

F32 = jnp.float32
BF16 = jnp.bfloat16

HEAD_DIM = 64
SLAB = 128
GROUP_W = 512
ROT_DIM = 16
ROT_HALF = ROT_DIM // 2
ROPE_THETA = 500000.0
NORM_EPS = 1e-6
SUBLN_EPS = 1e-5
NEG_INF = -1e30
LAM_INIT = 0.8 - 0.6 * math.exp(-0.3 * 0)
LOG2E = math.log2(math.e)
Q_SCALE = HEAD_DIM ** -0.5 * LOG2E

PROJ_ROWS = 512
OUT_ROWS = 512
Q_ROWS = 256
K_ROWS = 256
CHUNK_PAGES = 8
VMEM_LIMIT = 56 * 1024 * 1024

NT_DIMS = (((1,), (1,)), ((), ()))


def _softplus2(t):
    return jnp.maximum(t, 0.0) + jnp.log(1.0 + jnp.exp2(-jnp.abs(t))) * LOG2E


def _silu(x):
    return x / (1.0 + jnp.exp(-x))


def _tri2(n):
    j = lax.broadcasted_iota(jnp.int32, (2 * n, n), 0) % n
    s = lax.broadcasted_iota(jnp.int32, (2 * n, n), 1)
    return jnp.where(j >= s, 1.0, 0.0).astype(BF16)


def _rev_cumsum(x, tri2):
    hi = x.astype(BF16)
    lo = (x - hi.astype(F32)).astype(BF16)
    return jnp.dot(jnp.concatenate([hi, lo], axis=1), tri2, preferred_element_type=F32)


def _lam(lam_ref):
    lv = lam_ref[...]
    s1 = jnp.sum(lv[0:1] * lv[1:2], axis=1, keepdims=True)
    s2 = jnp.sum(lv[2:3] * lv[3:4], axis=1, keepdims=True)
    return jnp.exp(s1) - jnp.exp(s2) + LAM_INIT


def _normed(x_ref, g_ref):
    x = x_ref[...]
    ms = jnp.mean(x * x, axis=-1, keepdims=True)
    return ((x * lax.rsqrt(ms + NORM_EPS)) * g_ref[...]).astype(BF16)


def _rope_rows(p, cos, sin_lo, sin_hi):
    slabs = []
    for s in range(p.shape[1] // SLAB):
        xs = p[:, s * SLAB:(s + 1) * SLAB]
        slabs.append(xs * cos + pltpu.roll(xs, SLAB - ROT_HALF, axis=1) * sin_lo
                     + pltpu.roll(xs, ROT_HALF, axis=1) * sin_hi)
    return jnp.concatenate(slabs, axis=1)


def _proj_prompt_kernel(x_ref, g_ref, w_ref, wt_ref, cos_ref, sin_lo_ref, sin_hi_ref,
                        cos_t_ref, sin_t_ref,
                        skt_f, svt_f, dkt_f, dv_f, sq, skt, sv, dq, dkt, dv, gate):
    h = _normed(x_ref, g_ref)

    def proj(j):
        return jnp.dot(h, w_ref[:, j * GROUP_W:(j + 1) * GROUP_W], preferred_element_type=F32)

    def proj_t(j):
        return lax.dot_general(wt_ref[j * GROUP_W:(j + 1) * GROUP_W, :], h, NT_DIMS,
                               preferred_element_type=F32)

    def store_blocks(ref, kt):
        for kb in range(kt.shape[1] // K_ROWS):
            ref[0, kb] = kt[:, kb * K_ROWS:(kb + 1) * K_ROWS].astype(BF16)

    sq[...] = (proj(0) * Q_SCALE).astype(BF16)
    kt = proj_t(0)
    skt_f[0] = kt
    store_blocks(skt, kt)
    svt_f[0] = proj_t(1)
    sv[...] = proj(1).astype(BF16)
    gate[:, 0:GROUP_W] = _silu(proj(2)).astype(BF16)

    dq[...] = (_rope_rows(proj(3), cos_ref[...], sin_lo_ref[...], sin_hi_ref[...])
               * Q_SCALE).astype(BF16)
    kt = proj_t(2)
    cos_t, sin_t = cos_t_ref[...], sin_t_ref[...]
    parts = []
    for sh in range(GROUP_W // HEAD_DIM):
        base = sh * HEAD_DIM
        x1 = kt[base:base + ROT_HALF]
        x2 = kt[base + ROT_HALF:base + ROT_DIM]
        parts += [x1 * cos_t - x2 * sin_t, x2 * cos_t + x1 * sin_t,
                  kt[base + ROT_DIM:base + HEAD_DIM]]
    kt = jnp.concatenate(parts, axis=0)
    dkt_f[0] = kt
    store_blocks(dkt, kt)
    v = proj(4)
    dv_f[...] = v
    dv[...] = v.astype(BF16)
    gate[:, GROUP_W:2 * GROUP_W] = _silu(proj(5)).astype(BF16)


def _proj_sample_kernel(x_ref, g_ref, w_ref, cos_ref, sin_lo_ref, sin_hi_ref,
                        sq, sk, sv, dq, dk, dv, gate):
    h = _normed(x_ref, g_ref)

    def proj(j):
        return jnp.dot(h, w_ref[:, j * GROUP_W:(j + 1) * GROUP_W], preferred_element_type=F32)

    rope = lambda p: _rope_rows(p, cos_ref[...], sin_lo_ref[...], sin_hi_ref[...])
    sq[...] = proj(0) * Q_SCALE
    sk[...] = proj(1)
    sv[...] = proj(2)
    gate[:, 0:GROUP_W] = _silu(proj(3))
    dq[...] = rope(proj(4)) * Q_SCALE
    dk[...] = rope(proj(5))
    dv[...] = proj(6)
    gate[:, GROUP_W:2 * GROUP_W] = _silu(proj(7))


def _rope_angles(pos):
    inv = ROPE_THETA ** (-jnp.arange(ROT_HALF, dtype=F32) * 2.0 / ROT_DIM)
    ang = pos.astype(F32)[:, None] * inv[None, :]
    return jnp.cos(ang), jnp.sin(ang)


def _rope_tables(pos):
    cos, sin = _rope_angles(pos)
    n = pos.shape[0]
    ones = jnp.ones((n, HEAD_DIM - ROT_DIM), F32)
    zeros = jnp.zeros((n, HEAD_DIM - ROT_DIM), F32)
    zh = jnp.zeros((n, ROT_HALF), F32)
    cos64 = jnp.concatenate([cos, cos, ones], axis=1)
    lo64 = jnp.concatenate([-sin, zh, zeros], axis=1)
    hi64 = jnp.concatenate([zh, sin, zeros], axis=1)
    dup = lambda t: jnp.concatenate([t, t], axis=1)
    return dup(cos64), dup(lo64), dup(hi64)


def _project_prompt(x2d, norm_g, w_in, batch, seq):
    m = x2d.shape[0]
    tm = PROJ_ROWS
    nb = seq // tm
    nkb = seq // K_ROWS
    g = lambda j: w_in[:, j * GROUP_W:(j + 1) * GROUP_W]
    w_rows = jnp.concatenate([g(0), g(2), g(3), g(4), g(6), g(7)], axis=1).astype(BF16)
    w_t = jnp.concatenate([g(1), g(2), g(5)], axis=1).T.astype(BF16)
    pos = jnp.arange(seq)
    cos, sin = _rope_angles(pos)
    row = lambda i: (i, 0)
    full = lambda i: (0, 0)
    tab = pl.BlockSpec((tm, SLAB), lambda i: (i % nb, 0))
    tab_t = pl.BlockSpec((ROT_HALF, tm), lambda i: (0, i % nb))
    feat_major = jax.ShapeDtypeStruct((batch, GROUP_W, seq), F32)
    feat_spec = pl.BlockSpec((1, GROUP_W, tm), lambda i: (i // nb, 0, i % nb))
    kt_shape = jax.ShapeDtypeStruct((batch, nkb, GROUP_W, K_ROWS), BF16)
    kt_spec = pl.BlockSpec((1, tm // K_ROWS, GROUP_W, K_ROWS), lambda i: (i // nb, i % nb, 0, 0))
    b512 = jax.ShapeDtypeStruct((m, GROUP_W), BF16)
    blk512 = pl.BlockSpec((tm, GROUP_W), row)
    return pl.pallas_call(
        _proj_prompt_kernel,
        grid=(m // tm,),
        in_specs=[pl.BlockSpec((tm, 1024), row),
                  pl.BlockSpec((1, 1024), full),
                  pl.BlockSpec(w_rows.shape, full),
                  pl.BlockSpec(w_t.shape, full),
                  tab, tab, tab, tab_t, tab_t],
        out_specs=[feat_spec, feat_spec, feat_spec, blk512,
                   blk512, kt_spec, blk512, blk512, kt_spec, blk512,
                   pl.BlockSpec((tm, 2 * GROUP_W), row)],
        out_shape=[feat_major, feat_major, feat_major, jax.ShapeDtypeStruct((m, GROUP_W), F32),
                   b512, kt_shape, b512, b512, kt_shape, b512,
                   jax.ShapeDtypeStruct((m, 2 * GROUP_W), BF16)],
        compiler_params=pltpu.CompilerParams(
            dimension_semantics=("arbitrary",), vmem_limit_bytes=VMEM_LIMIT),
        name="proj_prompt",
    )(x2d, norm_g, w_rows, w_t, *_rope_tables(pos), cos.T, sin.T)


def _project_sample(x2d, norm_g, w_in, pos):
    m = x2d.shape[0]
    tm = PROJ_ROWS
    row = lambda i: (i, 0)
    full = lambda i: (0, 0)
    tab = pl.BlockSpec((tm, SLAB), full)
    f512 = jax.ShapeDtypeStruct((m, GROUP_W), F32)
    blk512 = pl.BlockSpec((tm, GROUP_W), row)
    return pl.pallas_call(
        _proj_sample_kernel,
        grid=(m // tm,),
        in_specs=[pl.BlockSpec((tm, 1024), row),
                  pl.BlockSpec((1, 1024), full),
                  pl.BlockSpec((1024, 8 * GROUP_W), full),
                  tab, tab, tab],
        out_specs=[blk512] * 6 + [pl.BlockSpec((tm, 2 * GROUP_W), row)],
        out_shape=[f512] * 6 + [jax.ShapeDtypeStruct((m, 2 * GROUP_W), F32)],
        compiler_params=pltpu.CompilerParams(
            dimension_semantics=("arbitrary",), vmem_limit_bytes=VMEM_LIMIT),
        name="proj_sample",
    )(x2d, norm_g, w_in.astype(BF16), *_rope_tables(pos))


def _prompt_kernel(qs_ref, kts_ref, vs_ref, qd_ref, ktd_ref, vd_ref, gs_ref, gd_ref,
                   lam_ref, subg_ref, os_ref, od_ref, acc_ref):
    tq = tk = Q_ROWS
    kd = pl.program_id(2)
    lane = lax.broadcasted_iota(jnp.int32, (tq, SLAB), 1)
    first = lane < HEAD_DIM
    q_pos = lax.broadcasted_iota(jnp.int32, (2 * tq, tk), 0) % tq
    k_pos = lax.broadcasted_iota(jnp.int32, (2 * tq, tk), 1)
    tri2 = _tri2(tk)
    zero_q = jnp.zeros((tq, SLAB), BF16)

    def both_halves(q):
        return jnp.concatenate([jnp.where(first, q, zero_q), jnp.where(first, zero_q, q)], axis=0)

    q_sb, q_df = both_halves(qs_ref[0]), both_halves(qd_ref[0])
    acc_ref[...] = jnp.zeros_like(acc_ref)

    def keys(ref, kb):
        return ref[0, pl.ds(pl.multiple_of(kb * tk, tk), tk), :]

    def step(kb, st, sb_mask, df_mask):
        carry, m, l = st
        t = jnp.dot(q_sb, kts_ref[0, kb], preferred_element_type=F32)
        u = jnp.dot(q_df, ktd_ref[0, kb], preferred_element_type=F32)
        s2 = _softplus2(t)
        if sb_mask is not None:
            s2 = jnp.where(sb_mask, s2, 0.0)
            u = jnp.where(df_mask, u, NEG_INF)
        cs = _rev_cumsum(s2, tri2)
        m_new = jnp.maximum(m, jnp.max(u, axis=1, keepdims=True))
        p = jnp.exp2(u - m_new)
        alpha = jnp.exp2(m - m_new)
        pv_df = jnp.dot(p.astype(BF16), keys(vd_ref, kb), preferred_element_type=F32)
        w = jnp.exp2(t - cs)
        if sb_mask is not None:
            w = jnp.where(sb_mask, w, 0.0)
        pv_sb = jnp.dot(w.astype(BF16), keys(vs_ref, kb), preferred_element_type=F32)
        acc_ref[1] = alpha * acc_ref[1] + pv_df
        acc_ref[0] += jnp.exp2(-carry) * pv_sb
        return carry + cs[:, 0:1], m_new, alpha * l + jnp.sum(p, axis=1, keepdims=True)

    col = lambda v: jnp.full((2 * tq, 1), v, F32)
    st = step(kd, (col(0.0), col(NEG_INF), col(0.0)), k_pos < q_pos, k_pos <= q_pos)
    st = lax.fori_loop(0, kd, lambda it, st: step(kd - 1 - it, st, None, None), st)
    _, _, l = st

    o_sb = jnp.where(first, acc_ref[0, 0:tq], acc_ref[0, tq:2 * tq])
    os_ref[0] = (o_sb * gs_ref[0].astype(F32)).astype(os_ref.dtype)

    o_df = acc_ref[1] / l
    o = o_df[0:tq] - _lam(lam_ref) * o_df[tq:2 * tq]
    o = (o * lax.rsqrt(jnp.mean(o * o, axis=-1, keepdims=True) + SUBLN_EPS)) * subg_ref[...]
    o = o * (1.0 - LAM_INIT)
    od_ref[0] = (o * gd_ref[0].astype(F32)).astype(od_ref.dtype)


def _prompt_attention(q_sb, kt_sb, v_sb, q_df, kt_df, v_df, gate, lamv, subg, batch, seq):
    tq = Q_ROWS
    assert Q_ROWS == K_ROWS
    nkb = seq // K_ROWS
    r3 = lambda a: a.reshape(batch, seq, a.shape[-1])
    qspec =pl.BlockSpec((1, tq, SLAB), lambda b, j, i: (b, i, j))
    ktspec = pl.BlockSpec((1, nkb, SLAB, K_ROWS), lambda b, j, i: (b, 0, j, 0))
    vspec = pl.BlockSpec((1, seq, SLAB), lambda b, j, i: (b, 0, j))
    n_slab = GROUP_W // SLAB
    out = jax.ShapeDtypeStruct((batch, seq, GROUP_W), BF16)
    return pl.pallas_call(
        _prompt_kernel,
        grid=(batch, n_slab, seq // tq),
        in_specs=[qspec, ktspec, vspec, qspec, ktspec, vspec,
                  qspec,
                  pl.BlockSpec((1, tq, SLAB), lambda b, j, i: (b, i, n_slab + j)),
                  pl.BlockSpec((4, HEAD_DIM), lambda b, j, i: (0, 0)),
                  pl.BlockSpec((1, SLAB), lambda b, j, i: (0, 0))],
        out_specs=[qspec, qspec],
        out_shape=[out, out],
        scratch_shapes=[pltpu.VMEM((2, 2 * tq, SLAB), F32)],
        compiler_params=pltpu.CompilerParams(
            dimension_semantics=("arbitrary", "arbitrary", "arbitrary"),
            vmem_limit_bytes=VMEM_LIMIT),
        name="prompt_attn",
    )(r3(q_sb), kt_sb, r3(v_sb), r3(q_df), kt_df, r3(v_df), r3(gate), r3(gate), lamv, subg)


def _sample_kernel(pt_ref, qs_ref, qd_ref, ksn_ref, vsn_ref, kdn_ref, vdn_ref, gate_ref,
                   lam_ref, subg_ref, c_sk, c_sv, c_dk, c_dv, os_ref, od_ref,
                   kt_sb, vt_sb, kt_df, v_df, sem, acc_sb, acc_df, *, n_seq, n_pages, page):
    g_pages = CHUNK_PAGES
    n_chunks = n_pages // g_pages
    ck = g_pages * page
    n_tok = qs_ref.shape[1]
    n_grp = GROUP_W // HEAD_DIM
    n_dfh = GROUP_W // SLAB
    rows = n_grp * n_tok
    s = pl.program_id(0)

    def chunk_copies(seq, c, slot):
        base = n_pages - (c + 1) * g_pages
        for g in range(g_pages):
            pg = pt_ref[seq, base + g]
            yield pltpu.make_async_copy(c_sk.at[pg], kt_sb.at[slot, g], sem.at[slot, 0, g])
            yield pltpu.make_async_copy(c_sv.at[pg], vt_sb.at[slot, g], sem.at[slot, 1, g])
            yield pltpu.make_async_copy(c_dk.at[pg], kt_df.at[slot, g], sem.at[slot, 2, g])
            yield pltpu.make_async_copy(c_dv.at[pg],
                                        v_df.at[slot, pl.ds(g * page * n_dfh, page * n_dfh), :],
                                        sem.at[slot, 3, g])

    def start_chunk(seq, c, slot):
        for cp in chunk_copies(seq, c, slot):
            cp.start()

    def wait_chunk(seq, c, slot):
        for cp in chunk_copies(seq, c, slot):
            cp.wait()

    @pl.when(s == 0)
    def _():
        start_chunk(0, 0, 0)

    row = lax.broadcasted_iota(jnp.int32, (rows, GROUP_W), 0)
    lane = lax.broadcasted_iota(jnp.int32, (rows, GROUP_W), 1)
    own64 = (row // n_tok) == (lane // HEAD_DIM)

    def block_diag(q_ref):
        q = q_ref[0]
        qt = jnp.concatenate([q] * n_grp, axis=0)
        return jnp.where(own64, qt, 0.0).astype(BF16)

    q_sb = block_diag(qs_ref)
    q_df = block_diag(qd_ref)

    def sb_weights(t, tri2, carry, mask):
        blk = tri2.shape[1]
        s2 = _softplus2(t)
        if mask is not None:
            s2 = jnp.where(mask, s2, 0.0)
        n_blk = t.shape[1] // blk
        sl = lambda a, j: a[:, j * blk:(j + 1) * blk]
        cs = [_rev_cumsum(sl(s2, j), tri2) for j in range(n_blk)]
        ws = [None] * n_blk
        for j in reversed(range(n_blk)):
            ws[j] = jnp.exp2(sl(t, j) - cs[j] - carry)
            carry = carry + cs[j][:, 0:1]
        w = ws[0] if n_blk == 1 else jnp.concatenate(ws, axis=1)
        if mask is not None:
            w = jnp.where(mask, w, 0.0)
        return w.astype(BF16), carry

    def df_weights(t, m, l, mask):
        if mask is not None:
            t = jnp.where(mask, t, NEG_INF)
        m_new = jnp.maximum(m, jnp.max(t, axis=1, keepdims=True))
        p = jnp.exp2(t - m_new)
        alpha = jnp.exp2(m - m_new)
        l = alpha * l + jnp.sum(p, axis=1, keepdims=True)
        acc_df[...] = alpha * acc_df[...]
        return p.astype(BF16), m_new, l

    def df_values(p, head_values):
        for hd in range(n_dfh):
            r = pl.ds(hd * 2 * n_tok, 2 * n_tok)
            acc_df[r, :] += jnp.dot(p[hd * 2 * n_tok:(hd + 1) * 2 * n_tok], head_values(hd),
                                    preferred_element_type=F32)

    def padded(ref):
        return jnp.concatenate([ref[0], jnp.zeros((page - n_tok, GROUP_W), F32)],
                               axis=0).astype(BF16)

    acc_sb[...] = jnp.zeros_like(acc_sb)
    acc_df[...] = jnp.zeros_like(acc_df)
    t_q = lax.broadcasted_iota(jnp.int32, (rows, page), 0) % n_tok
    s_k = lax.broadcasted_iota(jnp.int32, (rows, page), 1)
    t = lax.dot_general(q_sb, padded(ksn_ref), NT_DIMS, preferred_element_type=F32)
    w, carry = sb_weights(t, _tri2(page), jnp.zeros((rows, 1), F32), s_k < t_q)
    acc_sb[...] += jnp.dot(w, padded(vsn_ref), preferred_element_type=F32)
    t = lax.dot_general(q_df, padded(kdn_ref), NT_DIMS, preferred_element_type=F32)
    p, m, l = df_weights(t, jnp.full((rows, 1), NEG_INF, F32), jnp.zeros((rows, 1), F32),
                         s_k <= t_q)
    v_new = padded(vdn_ref)
    df_values(p, lambda hd: v_new[:, hd * SLAB:(hd + 1) * SLAB])

    tri2 = _tri2(K_ROWS)

    def body(c, st):
        carry, m, l = st
        slot = c % 2
        wait_chunk(s, c, slot)

        @pl.when(c + 1 < n_chunks)
        def _():
            start_chunk(s, c + 1, 1 - slot)

        @pl.when(jnp.logical_and(c + 1 == n_chunks, s + 1 < n_seq))
        def _():
            start_chunk(s + 1, 0, 1 - slot)

        def logits(q, kt):
            return jnp.concatenate(
                [jnp.dot(q, kt[slot, g].astype(BF16), preferred_element_type=F32)
                 for g in range(g_pages)], axis=1)

        w, carry = sb_weights(logits(q_sb, kt_sb), tri2, carry, None)
        pv = lax.dot_general(w[:, 0:page], vt_sb[slot, 0].astype(BF16), NT_DIMS,
                             preferred_element_type=F32)
        for g in range(1, g_pages):
            pv += lax.dot_general(w[:, g * page:(g + 1) * page], vt_sb[slot, g].astype(BF16),
                                  NT_DIMS, preferred_element_type=F32)
        acc_sb[...] += pv
        p, m, l = df_weights(logits(q_df, kt_df), m, l, None)
        df_values(p, lambda hd: v_df[slot, pl.ds(hd, ck, stride=n_dfh), :].astype(BF16))
        return carry, m, l

    carry, m, l = lax.fori_loop(0, n_chunks, body, (carry, m, l))

    gate = gate_ref[0]
    o_full = jnp.where(own64, acc_sb[...], 0.0)
    o_sb = o_full[0:n_tok]
    for hd in range(1, n_grp):
        o_sb = o_sb + o_full[hd * n_tok:(hd + 1) * n_tok]
    os_ref[0] = o_sb * gate[:, 0:GROUP_W]

    o_all = acc_df[...] / l
    lam = _lam(lam_ref)
    slabs = []
    for hd in range(n_dfh):
        r0 = hd * 2 * n_tok
        oh = o_all[r0:r0 + n_tok] - lam * o_all[r0 + n_tok:r0 + 2 * n_tok]
        oh = (oh * lax.rsqrt(jnp.mean(oh * oh, axis=-1, keepdims=True) + SUBLN_EPS)) * subg_ref[...]
        slabs.append(oh * (1.0 - LAM_INIT))
    od_ref[0] = jnp.concatenate(slabs, axis=1) * gate[:, GROUP_W:2 * GROUP_W]


def _sample_attention(page_table, q_sb, q_df, k_sb, v_sb, k_df, v_df, gate, lamv, subg,
                      c_sk, c_sv, c_dk, c_dv, n_seq, n_tok):
    n_pages = page_table.shape[1]
    page = c_sk.shape[2]
    n_chunks = n_pages // CHUNK_PAGES
    assert n_pages % CHUNK_PAGES == 0 and n_chunks % 2 == 0
    ck = CHUNK_PAGES * page
    r3 = lambda a: a.reshape(n_seq, n_tok, a.shape[-1])
    tok = pl.BlockSpec((1, n_tok, GROUP_W), lambda s, pt: (s, 0, 0))
    hbm = pl.BlockSpec(memory_space=pl.ANY)
    rows = GROUP_W // HEAD_DIM * n_tok
    out = jax.ShapeDtypeStruct((n_seq, n_tok, GROUP_W), F32)
    grid_spec = pltpu.PrefetchScalarGridSpec(
        num_scalar_prefetch=1,
        grid=(n_seq,),
        in_specs=[tok, tok, tok, tok, tok, tok,
                  pl.BlockSpec((1, n_tok, 2 * GROUP_W), lambda s, pt: (s, 0, 0)),
                  pl.BlockSpec((4, HEAD_DIM), lambda s, pt: (0, 0)),
                  pl.BlockSpec((1, SLAB), lambda s, pt: (0, 0)),
                  hbm, hbm, hbm, hbm],
        out_specs=[tok, tok],
        scratch_shapes=[pltpu.VMEM((2, CHUNK_PAGES, GROUP_W, page), F32),
                        pltpu.VMEM((2, CHUNK_PAGES, GROUP_W, page), F32),
                        pltpu.VMEM((2, CHUNK_PAGES, GROUP_W, page), F32),
                        pltpu.VMEM((2, ck * (GROUP_W // SLAB), SLAB), F32),
                        pltpu.SemaphoreType.DMA((2, 4, CHUNK_PAGES)),
                        pltpu.VMEM((rows, GROUP_W), F32),
                        pltpu.VMEM((rows, SLAB), F32)])
    return pl.pallas_call(
        functools.partial(_sample_kernel, n_seq=n_seq, n_pages=n_pages, page=page),
        grid_spec=grid_spec,
        out_shape=[out, out],
        compiler_params=pltpu.CompilerParams(
            dimension_semantics=("arbitrary",), vmem_limit_bytes=VMEM_LIMIT),
        name="sample_attn",
    )(page_table, r3(q_sb), r3(q_df), r3(k_sb), r3(v_sb), r3(k_df), r3(v_df), r3(gate),
      lamv, subg, c_sk, c_sv, c_dk, c_dv)


def _out_kernel(ms_ref, md_ref, x_ref, w_ref, g_ref, y_ref):
    acc = jnp.dot(ms_ref[...].astype(BF16), w_ref[0:GROUP_W, :], preferred_element_type=F32)
    acc = acc + jnp.dot(md_ref[...].astype(BF16), w_ref[GROUP_W:2 * GROUP_W, :],
                        preferred_element_type=F32)
    xo = x_ref[...] + acc
    ms = jnp.mean(xo * xo, axis=-1, keepdims=True)
    y_ref[...] = (xo * lax.rsqrt(ms + NORM_EPS)) * g_ref[...]


def _out_project(mix_sb, mix_df, x2d, w_bf16, norm_g):
    m = x2d.shape[0]
    tm = OUT_ROWS
    row = lambda i: (i, 0)
    return pl.pallas_call(
        _out_kernel,
        grid=(m // tm,),
        in_specs=[pl.BlockSpec((tm, GROUP_W), row), pl.BlockSpec((tm, GROUP_W), row),
                  pl.BlockSpec((tm, 1024), row),
                  pl.BlockSpec((1024, 1024), lambda i: (0, 0)),
                  pl.BlockSpec((1, 1024), lambda i: (0, 0))],
        out_specs=pl.BlockSpec((tm, 1024), row),
        out_shape=jax.ShapeDtypeStruct((m, 1024), F32),
        compiler_params=pltpu.CompilerParams(
            dimension_semantics=("arbitrary",), vmem_limit_bytes=VMEM_LIMIT),
        name="out_proj",
    )(mix_sb, mix_df, x2d, w_bf16, norm_g)


def kernel(x_prompt, x_sample, cache_sb_k, cache_sb_v, cache_df_k, cache_df_v, page_table,
           norm_in_g, w_in, lambda_q1, lambda_k1, lambda_q2, lambda_k2, subln_g, w_out,
           norm_f_g):
    batch, seq, d = x_prompt.shape
    n_seq, n_tok, _ = x_sample.shape
    assert norm_in_g.shape[0] == 1, "single layer"
    n_pool, page = cache_sb_k.shape[1], cache_sb_k.shape[2]
    past = page_table.shape[1] * page

    w_out_b = w_out[0].astype(BF16)
    g_in = norm_in_g[0].reshape(1, d)
    g_f = norm_f_g.reshape(1, d)
    lamv = jnp.stack([lambda_q1[0], lambda_k1[0], lambda_q2[0], lambda_k2[0]])
    subg = subln_g[0].reshape(1, SLAB)

    xp = x_prompt.reshape(batch * seq, d)
    (p_skt, p_svt, p_dkt, p_dv, q_sb, kt_sb, v_sb, q_df, kt_df, v_df, gate) = _project_prompt(
        xp, g_in, w_in[0], batch, seq)
    mix_sb, mix_df = _prompt_attention(q_sb, kt_sb, v_sb, q_df, kt_df, v_df, gate, lamv, subg,
                                       batch, seq)
    y_prompt = _out_project(mix_sb.reshape(batch * seq, GROUP_W),
                            mix_df.reshape(batch * seq, GROUP_W),
                            xp, w_out_b, g_f).reshape(batch, seq, d)
    new_sb_k = jnp.transpose(p_skt.reshape(batch, 8, 64, seq), (0, 3, 1, 2))[None]
    new_sb_v = jnp.transpose(p_svt.reshape(batch, 8, 64, seq), (0, 3, 1, 2))[None]
    new_df_k = jnp.transpose(p_dkt.reshape(batch, 4, 2, 64, seq), (0, 4, 1, 2, 3))[None]
    new_df_v = p_dv.reshape(1, batch, seq, 4, 128)

    xs = x_sample.reshape(n_seq * n_tok, d)
    pos_s = past + jnp.arange(PROJ_ROWS) % n_tok
    (sq_sb, s_sk, s_sv, sq_df, s_dk, s_dv, s_gate) = _project_sample(xs, g_in, w_in[0], pos_s)
    c_sk = jnp.transpose(cache_sb_k[0], (0, 2, 3, 1)).reshape(n_pool, GROUP_W, page)
    c_sv = jnp.transpose(cache_sb_v[0], (0, 2, 3, 1)).reshape(n_pool, GROUP_W, page)
    c_dk = jnp.transpose(cache_df_k[0], (0, 2, 3, 4, 1)).reshape(n_pool, GROUP_W, page)
    c_dv = cache_df_v[0].reshape(n_pool, page * 4, SLAB)
    smix_sb, smix_df = _sample_attention(
        page_table, sq_sb, sq_df, s_sk, s_sv, s_dk, s_dv, s_gate, lamv, subg,
        c_sk, c_sv, c_dk, c_dv, n_seq, n_tok)
    y_sample = _out_project(smix_sb.reshape(n_seq * n_tok, GROUP_W),
                            smix_df.reshape(n_seq * n_tok, GROUP_W),
                            xs, w_out_b, g_f).reshape(n_seq, n_tok, d)

    return (y_prompt, y_sample, new_sb_k, new_sb_v, new_df_k, new_df_v,
            s_sk.reshape(1, n_seq, n_tok, 8, 64), s_sv.reshape(1, n_seq, n_tok, 8, 64),
            s_dk.reshape(1, n_seq, n_tok, 4, 2, 64), s_dv.reshape(1, n_seq, n_tok, 4, 128))
```

```python
import functools
import math

import jax
import jax.numpy as jnp
from jax import lax
from jax.experimental import pallas as pl
from jax.experimental.pallas import tpu as pltpu

F32 = jnp.float32
BF16 = jnp.bfloat16

HEAD_DIM = 64
SLAB = 128
GROUP_W = 512
ROT_DIM = 16
ROT_HALF = ROT_DIM // 2
ROPE_THETA = 500000.0
NORM_EPS = 1e-6
SUBLN_EPS = 1e-5
NEG_INF = -1e30
LAM_INIT = 0.8 - 0.6 * math.exp(-0.3 * 0)
LOG2E = math.log2(math.e)
Q_SCALE = HEAD_DIM ** -0.5 * LOG2E

PROJ_ROWS = 512
OUT_ROWS = 512
Q_ROWS = 256
K_ROWS = 256
CHUNK_PAGES = 8
SB_CHUNK_PAGES = 2
VMEM_LIMIT = 56 * 1024 * 1024

SB_DONE_BITS = 160.0

NT_DIMS = (((1,), (1,)), ((), ()))


def _softplus2(t):
    return jnp.maximum(t, 0.0) + jnp.log(1.0 + jnp.exp2(-jnp.abs(t))) * LOG2E


def _silu(x):
    return x / (1.0 + jnp.exp(-x))


def _tri2(n):
    j = lax.broadcasted_iota(jnp.int32, (2 * n, n), 0) % n
    s = lax.broadcasted_iota(jnp.int32, (2 * n, n), 1)
    return jnp.where(j >= s, 1.0, 0.0).astype(BF16)


def _rev_cumsum(x, tri2):
    hi = x.astype(BF16)
    lo = (x - hi.astype(F32)).astype(BF16)
    return jnp.dot(jnp.concatenate([hi, lo], axis=1), tri2, preferred_element_type=F32)


def _lam(lam_ref):
    lv = lam_ref[...]
    s1 = jnp.sum(lv[0:1] * lv[1:2], axis=1, keepdims=True)
    s2 = jnp.sum(lv[2:3] * lv[3:4], axis=1, keepdims=True)
    return jnp.exp(s1) - jnp.exp(s2) + LAM_INIT


def _normed(x_ref, g_ref):
    x = x_ref[...]
    ms = jnp.mean(x * x, axis=-1, keepdims=True)
    return ((x * lax.rsqrt(ms + NORM_EPS)) * g_ref[...]).astype(BF16)


def _rope_rows(p, cos, sin_lo, sin_hi):
    slabs = []
    for s in range(p.shape[1] // SLAB):
        xs = p[:, s * SLAB:(s + 1) * SLAB]
        slabs.append(xs * cos + pltpu.roll(xs, SLAB - ROT_HALF, axis=1) * sin_lo
                     + pltpu.roll(xs, ROT_HALF, axis=1) * sin_hi)
    return jnp.concatenate(slabs, axis=1)


def _proj_prompt_kernel(x_ref, g_ref, w_ref, wt_ref, cos_ref, sin_lo_ref, sin_hi_ref,
                        cos_t_ref, sin_t_ref,
                        skt_f, svt_f, dkt_f, dv_f, sq, skt, sv, dq, dkt, dv, gate):
    h = _normed(x_ref, g_ref)

    def proj(j):
        return jnp.dot(h, w_ref[:, j * GROUP_W:(j + 1) * GROUP_W], preferred_element_type=F32)

    def proj_t(j):
        return lax.dot_general(wt_ref[j * GROUP_W:(j + 1) * GROUP_W, :], h, NT_DIMS,
                               preferred_element_type=F32)

    def store_blocks(ref, kt):
        for kb in range(kt.shape[1] // K_ROWS):
            ref[0, kb] = kt[:, kb * K_ROWS:(kb + 1) * K_ROWS].astype(BF16)

    sq[...] = (proj(0) * Q_SCALE).astype(BF16)
    kt = proj_t(0)
    skt_f[0] = kt
    store_blocks(skt, kt)
    svt_f[0] = proj_t(1)
    sv[...] = proj(1).astype(BF16)
    gate[:, 0:GROUP_W] = _silu(proj(2)).astype(BF16)

    dq[...] = (_rope_rows(proj(3), cos_ref[...], sin_lo_ref[...], sin_hi_ref[...])
               * Q_SCALE).astype(BF16)
    kt = proj_t(2)
    cos_t, sin_t = cos_t_ref[...], sin_t_ref[...]
    parts = []
    for sh in range(GROUP_W // HEAD_DIM):
        base = sh * HEAD_DIM
        x1 = kt[base:base + ROT_HALF]
        x2 = kt[base + ROT_HALF:base + ROT_DIM]
        parts += [x1 * cos_t - x2 * sin_t, x2 * cos_t + x1 * sin_t,
                  kt[base + ROT_DIM:base + HEAD_DIM]]
    kt = jnp.concatenate(parts, axis=0)
    dkt_f[0] = kt
    store_blocks(dkt, kt)
    v = proj(4)
    dv_f[...] = v
    dv[...] = v.astype(BF16)
    gate[:, GROUP_W:2 * GROUP_W] = _silu(proj(5)).astype(BF16)


def _proj_sample_kernel(x_ref, g_ref, w_ref, cos_ref, sin_lo_ref, sin_hi_ref,
                        sq, sk, sv, dq, dk, dv, gate):
    h = _normed(x_ref, g_ref)

    def proj(j):
        return jnp.dot(h, w_ref[:, j * GROUP_W:(j + 1) * GROUP_W], preferred_element_type=F32)

    rope = lambda p: _rope_rows(p, cos_ref[...], sin_lo_ref[...], sin_hi_ref[...])
    sq[...] = proj(0) * Q_SCALE
    sk[...] = proj(1)
    sv[...] = proj(2)
    gate[:, 0:GROUP_W] = _silu(proj(3))
    dq[...] = rope(proj(4)) * Q_SCALE
    dk[...] = rope(proj(5))
    dv[...] = proj(6)
    gate[:, GROUP_W:2 * GROUP_W] = _silu(proj(7))


def _rope_angles(pos):
    inv = ROPE_THETA ** (-jnp.arange(ROT_HALF, dtype=F32) * 2.0 / ROT_DIM)
    ang = pos.astype(F32)[:, None] * inv[None, :]
    return jnp.cos(ang), jnp.sin(ang)


def _rope_tables(pos):
    cos, sin = _rope_angles(pos)
    n = pos.shape[0]
    ones = jnp.ones((n, HEAD_DIM - ROT_DIM), F32)
    zeros = jnp.zeros((n, HEAD_DIM - ROT_DIM), F32)
    zh = jnp.zeros((n, ROT_HALF), F32)
    cos64 = jnp.concatenate([cos, cos, ones], axis=1)
    lo64 = jnp.concatenate([-sin, zh, zeros], axis=1)
    hi64 = jnp.concatenate([zh, sin, zeros], axis=1)
    dup = lambda t: jnp.concatenate([t, t], axis=1)
    return dup(cos64), dup(lo64), dup(hi64)


def _project_prompt(x2d, norm_g, w_in, batch, seq):
    m = x2d.shape[0]
    tm = PROJ_ROWS
    nb = seq // tm
    nkb = seq // K_ROWS
    g = lambda j: w_in[:, j * GROUP_W:(j + 1) * GROUP_W]
    w_rows = jnp.concatenate([g(0), g(2), g(3), g(4), g(6), g(7)], axis=1).astype(BF16)
    w_t = jnp.concatenate([g(1), g(2), g(5)], axis=1).T.astype(BF16)
    pos = jnp.arange(seq)
    cos, sin = _rope_angles(pos)
    row = lambda i: (i, 0)
    full = lambda i: (0, 0)
    tab = pl.BlockSpec((tm, SLAB), lambda i: (i % nb, 0))
    tab_t = pl.BlockSpec((ROT_HALF, tm), lambda i: (0, i % nb))
    feat_major = jax.ShapeDtypeStruct((batch, GROUP_W, seq), F32)
    feat_spec = pl.BlockSpec((1, GROUP_W, tm), lambda i: (i // nb, 0, i % nb))
    kt_shape = jax.ShapeDtypeStruct((batch, nkb, GROUP_W, K_ROWS), BF16)
    kt_spec = pl.BlockSpec((1, tm // K_ROWS, GROUP_W, K_ROWS), lambda i: (i // nb, i % nb, 0, 0))
    b512 = jax.ShapeDtypeStruct((m, GROUP_W), BF16)
    blk512 = pl.BlockSpec((tm, GROUP_W), row)
    return pl.pallas_call(
        _proj_prompt_kernel,
        grid=(m // tm,),
        in_specs=[pl.BlockSpec((tm, 1024), row),
                  pl.BlockSpec((1, 1024), full),
                  pl.BlockSpec(w_rows.shape, full),
                  pl.BlockSpec(w_t.shape, full),
                  tab, tab, tab, tab_t, tab_t],
        out_specs=[feat_spec, feat_spec, feat_spec, blk512,
                   blk512, kt_spec, blk512, blk512, kt_spec, blk512,
                   pl.BlockSpec((tm, 2 * GROUP_W), row)],
        out_shape=[feat_major, feat_major, feat_major, jax.ShapeDtypeStruct((m, GROUP_W), F32),
                   b512, kt_shape, b512, b512, kt_shape, b512,
                   jax.ShapeDtypeStruct((m, 2 * GROUP_W), BF16)],
        compiler_params=pltpu.CompilerParams(
            dimension_semantics=("arbitrary",), vmem_limit_bytes=VMEM_LIMIT),
        name="proj_prompt",
    )(x2d, norm_g, w_rows, w_t, *_rope_tables(pos), cos.T, sin.T)


def _project_sample(x2d, norm_g, w_in, pos):
    m = x2d.shape[0]
    tm = PROJ_ROWS
    row = lambda i: (i, 0)
    full = lambda i: (0, 0)
    tab = pl.BlockSpec((tm, SLAB), full)
    f512 = jax.ShapeDtypeStruct((m, GROUP_W), F32)
    blk512 = pl.BlockSpec((tm, GROUP_W), row)
    return pl.pallas_call(
        _proj_sample_kernel,
        grid=(m // tm,),
        in_specs=[pl.BlockSpec((tm, 1024), row),
                  pl.BlockSpec((1, 1024), full),
                  pl.BlockSpec((1024, 8 * GROUP_W), full),
                  tab, tab, tab],
        out_specs=[blk512] * 6 + [pl.BlockSpec((tm, 2 * GROUP_W), row)],
        out_shape=[f512] * 6 + [jax.ShapeDtypeStruct((m, 2 * GROUP_W), F32)],
        compiler_params=pltpu.CompilerParams(
            dimension_semantics=("arbitrary",), vmem_limit_bytes=VMEM_LIMIT),
        name="proj_sample",
    )(x2d, norm_g, w_in.astype(BF16), *_rope_tables(pos))


def _prompt_kernel(qs_ref, kts_ref, vs_ref, qd_ref, ktd_ref, vd_ref, gs_ref, gd_ref,
                   lam_ref, subg_ref, os_ref, od_ref, acc_ref):
    tq = tk = Q_ROWS
    kd = pl.program_id(2)
    lane = lax.broadcasted_iota(jnp.int32, (tq, SLAB), 1)
    first = lane < HEAD_DIM
    q_pos = lax.broadcasted_iota(jnp.int32, (2 * tq, tk), 0) % tq
    k_pos = lax.broadcasted_iota(jnp.int32, (2 * tq, tk), 1)
    tri2 = _tri2(tk)
    zero_q = jnp.zeros((tq, SLAB), BF16)

    def both_halves(q):
        return jnp.concatenate([jnp.where(first, q, zero_q), jnp.where(first, zero_q, q)], axis=0)

    q_sb, q_df = both_halves(qs_ref[0]), both_halves(qd_ref[0])
    acc_ref[...] = jnp.zeros_like(acc_ref)

    def keys(ref, kb):
        return ref[0, pl.ds(pl.multiple_of(kb * tk, tk), tk), :]

    def step(kb, st, sb_mask, df_mask):
        carry, m, l = st
        t = jnp.dot(q_sb, kts_ref[0, kb], preferred_element_type=F32)
        u = jnp.dot(q_df, ktd_ref[0, kb], preferred_element_type=F32)
        s2 = _softplus2(t)
        if sb_mask is not None:
            s2 = jnp.where(sb_mask, s2, 0.0)
            u = jnp.where(df_mask, u, NEG_INF)
        cs = _rev_cumsum(s2, tri2)
        m_new = jnp.maximum(m, jnp.max(u, axis=1, keepdims=True))
        p = jnp.exp2(u - m_new)
        alpha = jnp.exp2(m - m_new)
        pv_df = jnp.dot(p.astype(BF16), keys(vd_ref, kb), preferred_element_type=F32)
        w = jnp.exp2(t - cs)
        if sb_mask is not None:
            w = jnp.where(sb_mask, w, 0.0)
        pv_sb = jnp.dot(w.astype(BF16), keys(vs_ref, kb), preferred_element_type=F32)
        acc_ref[1] = alpha * acc_ref[1] + pv_df
        acc_ref[0] += jnp.exp2(-carry) * pv_sb
        return carry + cs[:, 0:1], m_new, alpha * l + jnp.sum(p, axis=1, keepdims=True)

    col = lambda v: jnp.full((2 * tq, 1), v, F32)
    st = step(kd, (col(0.0), col(NEG_INF), col(0.0)), k_pos < q_pos, k_pos <= q_pos)

    def sb_live(ls):
        return jnp.logical_and(ls[0] >= 0, ls[1] < SB_DONE_BITS)

    def both_body(ls):
        kb, _, st = ls
        st = step(kb, st, None, None)
        return kb - 1, jnp.min(st[0]), st

    kb, _, st = lax.while_loop(sb_live, both_body, (kd - 1, jnp.min(st[0]), st))

    def df_blocks(lo, n, ml):
        m, l = ml
        kt = [ktd_ref[0, lo + b] for b in range(n)]
        u = jnp.dot(q_df, kt[0] if n == 1 else jnp.concatenate(kt, axis=1),
                    preferred_element_type=F32)
        m_new = jnp.maximum(m, jnp.max(u, axis=1, keepdims=True))
        p = jnp.exp2(u - m_new)
        alpha = jnp.exp2(m - m_new)
        v = vd_ref[0, pl.ds(pl.multiple_of(lo * tk, tk), n * tk), :]
        acc_ref[1] = alpha * acc_ref[1] + jnp.dot(p.astype(BF16), v, preferred_element_type=F32)
        return m_new, alpha * l + jnp.sum(p, axis=1, keepdims=True)

    n_left = kb + 1
    ml = lax.fori_loop(0, n_left // 2, lambda it, ml: df_blocks(kb - 1 - 2 * it, 2, ml), st[1:])
    _, l = lax.fori_loop(0, n_left % 2, lambda it, ml: df_blocks(0, 1, ml), ml)

    o_sb = jnp.where(first, acc_ref[0, 0:tq], acc_ref[0, tq:2 * tq])
    os_ref[0] = (o_sb * gs_ref[0].astype(F32)).astype(os_ref.dtype)

    o_df = acc_ref[1] / l
    o = o_df[0:tq] - _lam(lam_ref) * o_df[tq:2 * tq]
    o = (o * lax.rsqrt(jnp.mean(o * o, axis=-1, keepdims=True) + SUBLN_EPS)) * subg_ref[...]
    o = o * (1.0 - LAM_INIT)
    od_ref[0] = (o * gd_ref[0].astype(F32)).astype(od_ref.dtype)


def _prompt_attention(q_sb, kt_sb, v_sb, q_df, kt_df, v_df, gate, lamv, subg, batch, seq):
    tq = Q_ROWS
    assert Q_ROWS == K_ROWS
    nkb = seq // K_ROWS
    r3 = lambda a: a.reshape(batch, seq, a.shape[-1])
    qspec =pl.BlockSpec((1, tq, SLAB), lambda b, j, i: (b, i, j))
    ktspec = pl.BlockSpec((1, nkb, SLAB, K_ROWS), lambda b, j, i: (b, 0, j, 0))
    vspec = pl.BlockSpec((1, seq, SLAB), lambda b, j, i: (b, 0, j))
    n_slab = GROUP_W // SLAB
    out = jax.ShapeDtypeStruct((batch, seq, GROUP_W), BF16)
    return pl.pallas_call(
        _prompt_kernel,
        grid=(batch, n_slab, seq // tq),
        in_specs=[qspec, ktspec, vspec, qspec, ktspec, vspec,
                  qspec,
                  pl.BlockSpec((1, tq, SLAB), lambda b, j, i: (b, i, n_slab + j)),
                  pl.BlockSpec((4, HEAD_DIM), lambda b, j, i: (0, 0)),
                  pl.BlockSpec((1, SLAB), lambda b, j, i: (0, 0))],
        out_specs=[qspec, qspec],
        out_shape=[out, out],
        scratch_shapes=[pltpu.VMEM((2, 2 * tq, SLAB), F32)],
        compiler_params=pltpu.CompilerParams(
            dimension_semantics=("arbitrary", "arbitrary", "arbitrary"),
            vmem_limit_bytes=VMEM_LIMIT),
        name="prompt_attn",
    )(r3(q_sb), kt_sb, r3(v_sb), r3(q_df), kt_df, r3(v_df), r3(gate), r3(gate), lamv, subg)


def _sample_kernel(pt_ref, qs_ref, qd_ref, ksn_ref, vsn_ref, kdn_ref, vdn_ref, gate_ref,
                   lam_ref, subg_ref, c_sk, c_sv, c_dk, c_dv, os_ref, od_ref,
                   kt_sb, vt_sb, kt_df, v_df, sem_sb, sem_df, acc_sb, acc_df,
                   *, n_seq, n_pages, page):
    n_tok = qs_ref.shape[1]
    n_grp = GROUP_W // HEAD_DIM
    n_dfh = GROUP_W // SLAB
    rows = n_grp * n_tok
    nc_sb = n_pages // SB_CHUNK_PAGES
    nc_df = n_pages // CHUNK_PAGES
    s = pl.program_id(0)

    def sb_copies(seq, c):
        base = n_pages - (c + 1) * SB_CHUNK_PAGES
        slot = c % 2
        for g in range(SB_CHUNK_PAGES):
            pg = pt_ref[seq, base + g]
            yield pltpu.make_async_copy(c_sk.at[pg], kt_sb.at[slot, g], sem_sb.at[slot, 0, g])
            yield pltpu.make_async_copy(c_sv.at[pg], vt_sb.at[slot, g], sem_sb.at[slot, 1, g])

    def df_copies(seq, c):
        base = n_pages - (c + 1) * CHUNK_PAGES
        slot = c % 2
        for g in range(CHUNK_PAGES):
            pg = pt_ref[seq, base + g]
            yield pltpu.make_async_copy(c_dk.at[pg], kt_df.at[slot, g], sem_df.at[slot, 0, g])
            yield pltpu.make_async_copy(c_dv.at[pg],
                                        v_df.at[slot, pl.ds(g * page * n_dfh, page * n_dfh), :],
                                        sem_df.at[slot, 1, g])

    def start(copies):
        for cp in copies:
            cp.start()

    def wait(copies):
        for cp in copies:
            cp.wait()

    @pl.when(s == 0)
    def _():
        start(sb_copies(0, 0))
        start(df_copies(0, 0))

    start(df_copies(s, 1))

    row = lax.broadcasted_iota(jnp.int32, (rows, GROUP_W), 0)
    lane = lax.broadcasted_iota(jnp.int32, (rows, GROUP_W), 1)
    own64 = (row // n_tok) == (lane // HEAD_DIM)

    def block_diag(q_ref):
        q = q_ref[0]
        qt = jnp.concatenate([q] * n_grp, axis=0)
        return jnp.where(own64, qt, 0.0).astype(BF16)

    q_sb = block_diag(qs_ref)
    q_df = block_diag(qd_ref)

    def sb_weights(t, tri2, carry, mask):
        blk = tri2.shape[1]
        s2 = _softplus2(t)
        if mask is not None:
            s2 = jnp.where(mask, s2, 0.0)
        n_blk = t.shape[1] // blk
        sl = lambda a, j: a[:, j * blk:(j + 1) * blk]
        cs = [_rev_cumsum(sl(s2, j), tri2) for j in range(n_blk)]
        ws = [None] * n_blk
        for j in reversed(range(n_blk)):
            ws[j] = jnp.exp2(sl(t, j) - cs[j] - carry)
            carry = carry + cs[j][:, 0:1]
        w = ws[0] if n_blk == 1 else jnp.concatenate(ws, axis=1)
        if mask is not None:
            w = jnp.where(mask, w, 0.0)
        return w.astype(BF16), carry

    def df_weights(t, m, l, mask):
        if mask is not None:
            t = jnp.where(mask, t, NEG_INF)
        m_new = jnp.maximum(m, jnp.max(t, axis=1, keepdims=True))
        p = jnp.exp2(t - m_new)
        alpha = jnp.exp2(m - m_new)
        l = alpha * l + jnp.sum(p, axis=1, keepdims=True)
        acc_df[...] = alpha * acc_df[...]
        return p.astype(BF16), m_new, l

    def df_values(p, head_values):
        for hd in range(n_dfh):
            r = pl.ds(hd * 2 * n_tok, 2 * n_tok)
            acc_df[r, :] += jnp.dot(p[hd * 2 * n_tok:(hd + 1) * 2 * n_tok], head_values(hd),
                                    preferred_element_type=F32)

    def padded(ref):
        return jnp.concatenate([ref[0], jnp.zeros((page - n_tok, GROUP_W), F32)],
                               axis=0).astype(BF16)

    acc_sb[...] = jnp.zeros_like(acc_sb)
    acc_df[...] = jnp.zeros_like(acc_df)
    t_q = lax.broadcasted_iota(jnp.int32, (rows, page), 0) % n_tok
    s_k = lax.broadcasted_iota(jnp.int32, (rows, page), 1)
    t = lax.dot_general(q_sb, padded(ksn_ref), NT_DIMS, preferred_element_type=F32)
    w, carry = sb_weights(t, _tri2(page), jnp.zeros((rows, 1), F32), s_k < t_q)
    acc_sb[...] += jnp.dot(w, padded(vsn_ref), preferred_element_type=F32)
    t = lax.dot_general(q_df, padded(kdn_ref), NT_DIMS, preferred_element_type=F32)
    p, m, l = df_weights(t, jnp.full((rows, 1), NEG_INF, F32), jnp.zeros((rows, 1), F32),
                         s_k <= t_q)
    v_new = padded(vdn_ref)
    df_values(p, lambda hd: v_new[:, hd * SLAB:(hd + 1) * SLAB])

    tri2 = _tri2(K_ROWS)

    def logits(q, kt, slot, n):
        return jnp.concatenate(
            [jnp.dot(q, kt[slot, g].astype(BF16), preferred_element_type=F32)
             for g in range(n)], axis=1)

    def sb_live(ls):
        return jnp.logical_and(ls[0] < nc_sb, ls[1] < SB_DONE_BITS)

    def sb_body(ls):
        c, _, carry = ls
        slot = c % 2
        wait(sb_copies(s, c))

        @pl.when(c + 1 < nc_sb)
        def _():
            start(sb_copies(s, c + 1))

        w, carry = sb_weights(logits(q_sb, kt_sb, slot, SB_CHUNK_PAGES), tri2, carry, None)
        pv = lax.dot_general(w[:, 0:page], vt_sb[slot, 0].astype(BF16), NT_DIMS,
                             preferred_element_type=F32)
        for g in range(1, SB_CHUNK_PAGES):
            pv += lax.dot_general(w[:, g * page:(g + 1) * page], vt_sb[slot, g].astype(BF16),
                                  NT_DIMS, preferred_element_type=F32)
        acc_sb[...] += pv
        return c + 1, jnp.min(carry), carry

    c_end, _, carry = lax.while_loop(sb_live, sb_body, (jnp.int32(0), jnp.min(carry), carry))

    @pl.when(c_end < nc_sb)
    def _():
        wait(sb_copies(s, c_end))

    @pl.when(s + 1 < n_seq)
    def _():
        start(sb_copies(s + 1, 0))

    ck = CHUNK_PAGES * page

    def df_body(c, st):
        m, l = st
        slot = c % 2
        wait(df_copies(s, c))
        p, m, l = df_weights(logits(q_df, kt_df, slot, CHUNK_PAGES), m, l, None)
        df_values(p, lambda hd: v_df[slot, pl.ds(hd, ck, stride=n_dfh), :].astype(BF16))

        @pl.when(c + 2 < nc_df)
        def _():
            start(df_copies(s, c + 2))

        @pl.when(jnp.logical_and(c + 2 == nc_df, s + 1 < n_seq))
        def _():
            start(df_copies(s + 1, 0))

        return m, l

    m, l = lax.fori_loop(0, nc_df, df_body, (m, l))

    gate = gate_ref[0]
    o_full = jnp.where(own64, acc_sb[...], 0.0)
    o_sb = o_full[0:n_tok]
    for hd in range(1, n_grp):
        o_sb = o_sb + o_full[hd * n_tok:(hd + 1) * n_tok]
    os_ref[0] = o_sb * gate[:, 0:GROUP_W]

    o_all = acc_df[...] / l
    lam = _lam(lam_ref)
    slabs = []
    for hd in range(n_dfh):
        r0 = hd * 2 * n_tok
        oh = o_all[r0:r0 + n_tok] - lam * o_all[r0 + n_tok:r0 + 2 * n_tok]
        oh = (oh * lax.rsqrt(jnp.mean(oh * oh, axis=-1, keepdims=True) + SUBLN_EPS)) * subg_ref[...]
        slabs.append(oh * (1.0 - LAM_INIT))
    od_ref[0] = jnp.concatenate(slabs, axis=1) * gate[:, GROUP_W:2 * GROUP_W]


def _sample_attention(page_table, q_sb, q_df, k_sb, v_sb, k_df, v_df, gate, lamv, subg,
                      c_sk, c_sv, c_dk, c_dv, n_seq, n_tok):
    n_pages = page_table.shape[1]
    page = c_sk.shape[2]
    nc_df = n_pages // CHUNK_PAGES
    assert n_pages % CHUNK_PAGES == 0 and nc_df % 2 == 0
    assert n_pages % SB_CHUNK_PAGES == 0 and (SB_CHUNK_PAGES * page) % K_ROWS == 0
    ck = CHUNK_PAGES * page
    r3 = lambda a: a.reshape(n_seq, n_tok, a.shape[-1])
    tok = pl.BlockSpec((1, n_tok, GROUP_W), lambda s, pt: (s, 0, 0))
    hbm = pl.BlockSpec(memory_space=pl.ANY)
    rows = GROUP_W // HEAD_DIM * n_tok
    out = jax.ShapeDtypeStruct((n_seq, n_tok, GROUP_W), F32)
    grid_spec = pltpu.PrefetchScalarGridSpec(
        num_scalar_prefetch=1,
        grid=(n_seq,),
        in_specs=[tok, tok, tok, tok, tok, tok,
                  pl.BlockSpec((1, n_tok, 2 * GROUP_W), lambda s, pt: (s, 0, 0)),
                  pl.BlockSpec((4, HEAD_DIM), lambda s, pt: (0, 0)),
                  pl.BlockSpec((1, SLAB), lambda s, pt: (0, 0)),
                  hbm, hbm, hbm, hbm],
        out_specs=[tok, tok],
        scratch_shapes=[pltpu.VMEM((2, SB_CHUNK_PAGES, GROUP_W, page), F32),
                        pltpu.VMEM((2, SB_CHUNK_PAGES, GROUP_W, page), F32),
                        pltpu.VMEM((2, CHUNK_PAGES, GROUP_W, page), F32),
                        pltpu.VMEM((2, ck * (GROUP_W // SLAB), SLAB), F32),
                        pltpu.SemaphoreType.DMA((2, 2, SB_CHUNK_PAGES)),
                        pltpu.SemaphoreType.DMA((2, 2, CHUNK_PAGES)),
                        pltpu.VMEM((rows, GROUP_W), F32),
                        pltpu.VMEM((rows, SLAB), F32)])
    return pl.pallas_call(
        functools.partial(_sample_kernel, n_seq=n_seq, n_pages=n_pages, page=page),
        grid_spec=grid_spec,
        out_shape=[out, out],
        compiler_params=pltpu.CompilerParams(
            dimension_semantics=("arbitrary",), vmem_limit_bytes=VMEM_LIMIT),
        name="sample_attn",
    )(page_table, r3(q_sb), r3(q_df), r3(k_sb), r3(v_sb), r3(k_df), r3(v_df), r3(gate),
      lamv, subg, c_sk, c_sv, c_dk, c_dv)


def _out_kernel(ms_ref, md_ref, x_ref, w_ref, g_ref, y_ref):
    acc = jnp.dot(ms_ref[...].astype(BF16), w_ref[0:GROUP_W, :], preferred_element_type=F32)
    acc = acc + jnp.dot(md_ref[...].astype(BF16), w_ref[GROUP_W:2 * GROUP_W, :],
                        preferred_element_type=F32)
    xo = x_ref[...] + acc
    ms = jnp.mean(xo * xo, axis=-1, keepdims=True)
    y_ref[...] = (xo * lax.rsqrt(ms + NORM_EPS)) * g_ref[...]


def _out_project(mix_sb, mix_df, x2d, w_bf16, norm_g):
    m = x2d.shape[0]
    tm = OUT_ROWS
    row = lambda i: (i, 0)
    return pl.pallas_call(
        _out_kernel,
        grid=(m // tm,),
        in_specs=[pl.BlockSpec((tm, GROUP_W), row), pl.BlockSpec((tm, GROUP_W), row),
                  pl.BlockSpec((tm, 1024), row),
                  pl.BlockSpec((1024, 1024), lambda i: (0, 0)),
                  pl.BlockSpec((1, 1024), lambda i: (0, 0))],
        out_specs=pl.BlockSpec((tm, 1024), row),
        out_shape=jax.ShapeDtypeStruct((m, 1024), F32),
        compiler_params=pltpu.CompilerParams(
            dimension_semantics=("arbitrary",), vmem_limit_bytes=VMEM_LIMIT),
        name="out_proj",
    )(mix_sb, mix_df, x2d, w_bf16, norm_g)


def kernel(x_prompt, x_sample, cache_sb_k, cache_sb_v, cache_df_k, cache_df_v, page_table,
           norm_in_g, w_in, lambda_q1, lambda_k1, lambda_q2, lambda_k2, subln_g, w_out,
           norm_f_g):
    batch, seq, d = x_prompt.shape
    n_seq, n_tok, _ = x_sample.shape
    assert norm_in_g.shape[0] == 1, "single layer"
    n_pool, page = cache_sb_k.shape[1], cache_sb_k.shape[2]
    past = page_table.shape[1] * page

    w_out_b = w_out[0].astype(BF16)
    g_in = norm_in_g[0].reshape(1, d)
    g_f = norm_f_g.reshape(1, d)
    lamv = jnp.stack([lambda_q1[0], lambda_k1[0], lambda_q2[0], lambda_k2[0]])
    subg = subln_g[0].reshape(1, SLAB)

    xp = x_prompt.reshape(batch * seq, d)
    (p_skt, p_svt, p_dkt, p_dv, q_sb, kt_sb, v_sb, q_df, kt_df, v_df, gate) = _project_prompt(
        xp, g_in, w_in[0], batch, seq)
    mix_sb, mix_df = _prompt_attention(q_sb, kt_sb, v_sb, q_df, kt_df, v_df, gate, lamv, subg,
                                       batch, seq)
    y_prompt = _out_project(mix_sb.reshape(batch * seq, GROUP_W),
                            mix_df.reshape(batch * seq, GROUP_W),
                            xp, w_out_b, g_f).reshape(batch, seq, d)
    new_sb_k = jnp.transpose(p_skt.reshape(batch, 8, 64, seq), (0, 3, 1, 2))[None]
    new_sb_v = jnp.transpose(p_svt.reshape(batch, 8, 64, seq), (0, 3, 1, 2))[None]
    new_df_k = jnp.transpose(p_dkt.reshape(batch, 4, 2, 64, seq), (0, 4, 1, 2, 3))[None]
    new_df_v = p_dv.reshape(1, batch, seq, 4, 128)

    xs = x_sample.reshape(n_seq * n_tok, d)
    pos_s = past + jnp.arange(PROJ_ROWS) % n_tok
    (sq_sb, s_sk, s_sv, sq_df, s_dk, s_dv, s_gate) = _project_sample(xs, g_in, w_in[0], pos_s)
    c_sk = jnp.transpose(cache_sb_k[0], (0, 2, 3, 1)).reshape(n_pool, GROUP_W, page)
    c_sv = jnp.transpose(cache_sb_v[0], (0, 2, 3, 1)).reshape(n_pool, GROUP_W, page)
    c_dk = jnp.transpose(cache_df_k[0], (0, 2, 3, 4, 1)).reshape(n_pool, GROUP_W, page)
    c_dv = cache_df_v[0].reshape(n_pool, page * 4, SLAB)
    smix_sb, smix_df = _sample_attention(
        page_table, sq_sb, sq_df, s_sk, s_sv, s_dk, s_dv, s_gate, lamv, subg,
        c_sk, c_sv, c_dk, c_dv, n_seq, n_tok)
    y_sample = _out_project(smix_sb.reshape(n_seq * n_tok, GROUP_W),
                            smix_df.reshape(n_seq * n_tok, GROUP_W),
                            xs, w_out_b, g_f).reshape(n_seq, n_tok, d)

    return (y_prompt, y_sample, new_sb_k, new_sb_v, new_df_k, new_df_v,
            s_sk.reshape(1, n_seq, n_tok, 8, 64), s_sv.reshape(1, n_seq, n_tok, 8, 64),
            s_dk.reshape(1, n_seq, n_tok, 4, 2, 64), s_dv.reshape(1, n_seq, n_tok, 4, 128))
```
